```python
import math, functools
import jax, jax.numpy as jnp
from jax import lax
import numpy as np

D_MODEL = 4096
BATCH = 2
SEQ = 8192
DEPTH = 4

GRID_W = 64
CTX_LEN = 256
N_MIXERS = 2
Q_BLOCK = 128
ROPE_BASE = 10000.0
EPS = 1e-6
N_MOD = 6

MLA_HEADS = 32
MLA_Q_RANK = 1024
MLA_KV_RANK = 512
MLA_NOPE_DIM = 128
MLA_ROPE_DIM = 64
MLA_V_DIM = 128
MLA_QK_DIM = MLA_NOPE_DIM + MLA_ROPE_DIM
MLA_SCALE = MLA_QK_DIM ** -0.5

DIFF_HEAD_DIM = 128
DIFF_HEADS = D_MODEL // (2 * DIFF_HEAD_DIM)
DIFF_SCALE = DIFF_HEAD_DIM ** -0.5

FFN_DIM = 3072
N_EXPERTS = 8
TOP_K = 2
EXPERT_DIM = 1024

kernel_name = "hybrid_mla_diffattn_moe_dit"


def rms_norm(x, g):
    xf = x.astype(jnp.float32)
    y = xf * lax.rsqrt(jnp.mean(xf * xf, axis=-1, keepdims=True) + EPS)
    return (y * g.astype(jnp.float32)).astype(x.dtype)


def modulate(x, g, shift, scale):
    return rms_norm(x, g) * (1 + scale) + shift


def rope_tables(pos, dim):
    inv = ROPE_BASE ** (-jnp.arange(0, dim, 2, dtype=jnp.float32) / dim)
    ang = pos[:, None] * inv[None, :]
    return jnp.cos(ang), jnp.sin(ang)


def rope_rotate(x, cos, sin):
    shape = (cos.shape[0],) + (1,) * (x.ndim - 3) + (cos.shape[1],)
    cos = cos.reshape(shape).astype(x.dtype)
    sin = sin.reshape(shape).astype(x.dtype)
    x1, x2 = jnp.split(x, 2, axis=-1)
    return jnp.concatenate([x1 * cos - x2 * sin, x2 * cos + x1 * sin], axis=-1)


def axial_rope(x, row, col):
    half = x.shape[-1] // 2
    cr, sr = rope_tables(row, half)
    cc, scol = rope_tables(col, half)
    return jnp.concatenate([rope_rotate(x[..., :half], cr, sr),
                            rope_rotate(x[..., half:], cc, scol)], axis=-1)


def map_query_blocks(fn, queries):
    B, T = queries[0].shape[:2]
    nb = T // Q_BLOCK
    blocks = tuple(jnp.moveaxis(q.reshape((B, nb, Q_BLOCK) + q.shape[2:]), 1, 0) for q in queries)
    out = lax.map(lambda blk: fn(*blk), blocks)
    out = jnp.moveaxis(out, 0, 1)
    return out.reshape((B, T) + out.shape[3:])


def mla_queries(h, w_dq, q_norm, w_uq, rope):
    B, T, _ = h.shape
    q = (rms_norm(h @ w_dq, q_norm) @ w_uq).reshape(B, T, MLA_HEADS, MLA_QK_DIM)
    q_nope, q_pe = jnp.split(q, [MLA_NOPE_DIM], axis=-1)
    if rope is not None:
        q_pe = rope(q_pe)
    return q_nope, q_pe


def mla_keys_values(h, w_dkv, kv_norm, w_ukv, rope):
    B, T, _ = h.shape
    c_kv, k_pe = jnp.split(h @ w_dkv, [MLA_KV_RANK], axis=-1)
    kv = (rms_norm(c_kv, kv_norm) @ w_ukv).reshape(B, T, MLA_HEADS, MLA_NOPE_DIM + MLA_V_DIM)
    k_nope, v = jnp.split(kv, [MLA_NOPE_DIM], axis=-1)
    if rope is not None:
        k_pe = rope(k_pe)
    return k_nope, k_pe, v


def mla_attend(q_nope, q_pe, k_nope, k_pe, v):
    s = (jnp.einsum('bqhd,bkhd->bhqk', q_nope, k_nope)
         + jnp.einsum('bqhr,bkr->bhqk', q_pe, k_pe))
    p = jax.nn.softmax(s.astype(jnp.float32) * MLA_SCALE, axis=-1).astype(v.dtype)
    return jnp.einsum('bhqk,bkhd->bqhd', p, v)


def mla_mixer(h_lat, h_ctx, w_dq, q_norm, w_uq, w_dkv, kv_norm, w_ukv, w_o, rope, with_ctx_out):
    B, S, _ = h_lat.shape
    C = h_ctx.shape[1]
    kc, kpc, vc = mla_keys_values(h_ctx, w_dkv, kv_norm, w_ukv, None)
    kl, kpl, vl = mla_keys_values(h_lat, w_dkv, kv_norm, w_ukv, rope)
    k_all = jnp.concatenate([kc, kl], axis=1)
    kp_all = jnp.concatenate([kpc, kpl], axis=1)
    v_all = jnp.concatenate([vc, vl], axis=1)
    ql, qpl = mla_queries(h_lat, w_dq, q_norm, w_uq, rope)
    o_lat = map_query_blocks(lambda qn, qp: mla_attend(qn, qp, k_all, kp_all, v_all), (ql, qpl))
    y_lat = o_lat.reshape(B, S, MLA_HEADS * MLA_V_DIM) @ w_o
    if not with_ctx_out:
        return y_lat, None
    qc, qpc = mla_queries(h_ctx, w_dq, q_norm, w_uq, None)
    y_ctx = mla_attend(qc, qpc, kc, kpc, vc).reshape(B, C, MLA_HEADS * MLA_V_DIM) @ w_o
    return y_lat, y_ctx


def diff_queries(h, w_q, rope):
    B, T, _ = h.shape
    q = (h @ w_q).reshape(B, T, DIFF_HEADS, 2, DIFF_HEAD_DIM)
    return q if rope is None else rope(q)


def diff_keys_values(h, w_kv, rope):
    B, T, _ = h.shape
    k, v = jnp.split(h @ w_kv, 2, axis=-1)
    k = k.reshape(B, T, DIFF_HEADS, 2, DIFF_HEAD_DIM)
    v = v.reshape(B, T, DIFF_HEADS, 2 * DIFF_HEAD_DIM)
    if rope is not None:
        k = rope(k)
    return k, v


def diff_attend(q, k, v, lam):
    s = jnp.einsum('bqhnd,bkhnd->bhnqk', q, k).astype(jnp.float32) * DIFF_SCALE
    p = jax.nn.softmax(s, axis=-1)
    a = (p[:, :, 0] - lam * p[:, :, 1]).astype(v.dtype)
    return jnp.einsum('bhqk,bkhe->bqhe', a, v)


def diff_mixer(h_lat, h_ctx, w_qkv, lam_params, subln, w_o, lambda_init, rope, with_ctx_out):
    B, S, D = h_lat.shape
    w_q, w_kv = w_qkv[:, :D], w_qkv[:, D:]
    lp = lam_params.astype(jnp.float32)
    lam = jnp.exp(jnp.sum(lp[0] * lp[1])) - jnp.exp(jnp.sum(lp[2] * lp[3])) + lambda_init
    kc, vc = diff_keys_values(h_ctx, w_kv, None)
    kl, vl = diff_keys_values(h_lat, w_kv, rope)
    k_all = jnp.concatenate([kc, kl], axis=1)
    v_all = jnp.concatenate([vc, vl], axis=1)

    def finish(o):
        o = rms_norm(o, subln) * (1.0 - lambda_init)
        return o.reshape(o.shape[0], o.shape[1], D) @ w_o

    ql = diff_queries(h_lat, w_q, rope)
    y_lat = finish(map_query_blocks(lambda q: diff_attend(q, k_all, v_all, lam), (ql,)))
    if not with_ctx_out:
        return y_lat, None
    qc = diff_queries(h_ctx, w_q, None)
    y_ctx = finish(diff_attend(qc, kc, vc, lam))
    return y_lat, y_ctx


def swiglu(h, w_gu, w_down):
    g, u = jnp.split(h @ w_gu, 2, axis=-1)
    return (jax.nn.silu(g) * u) @ w_down


def moe_ffn(h, w_router, w_gu, w_down):
    logits = (h @ w_router).astype(jnp.float32)
    top_val, top_idx = lax.top_k(logits, TOP_K)
    top_w = jax.nn.softmax(top_val, axis=-1)
    combine = jnp.sum(jax.nn.one_hot(top_idx, N_EXPERTS, dtype=jnp.float32) * top_w[..., None], axis=-2)
    y = jnp.zeros_like(h)
    for e in range(N_EXPERTS):
        y = y + combine[..., e:e + 1].astype(h.dtype) * swiglu(h, w_gu[e], w_down[e])
    return y


def setup_inputs(seed: int = 0) -> dict:
    key = jax.random.key(seed)
    ks = iter(jax.random.split(key, 24))
    L, La, Lb = DEPTH, (DEPTH + 1) // 2, DEPTH // 2
    D = D_MODEL

    def w(shape, fan_in, gain=1.0):
        return jax.random.normal(next(ks), shape, jnp.float32) * (gain * fan_in ** -0.5)

    def gains(shape):
        return 1.0 + 0.02 * jax.random.normal(next(ks), shape, jnp.float32)

    return {
        "x": jax.random.normal(next(ks), (BATCH, SEQ, D), jnp.float32),
        "c": jax.random.normal(next(ks), (BATCH, D), jnp.float32),
        "ctx": jax.random.normal(next(ks), (BATCH, CTX_LEN, D), jnp.float32),
        "c_ctx": jax.random.normal(next(ks), (D,), jnp.float32),
        "ada_w": w((L, D, N_MOD * D), D, 0.5),
        "ada_b": 0.02 * jax.random.normal(next(ks), (L, N_MOD * D), jnp.float32),
        "norm_g": gains((L, 4, D)),
        "mla_w_dq": w((La, D, MLA_Q_RANK), D),
        "mla_q_norm": gains((La, MLA_Q_RANK)),
        "mla_w_uq": w((La, MLA_Q_RANK, MLA_HEADS * MLA_QK_DIM), MLA_Q_RANK),
        "mla_w_dkv": w((La, D, MLA_KV_RANK + MLA_ROPE_DIM), D),
        "mla_kv_norm": gains((La, MLA_KV_RANK)),
        "mla_w_ukv": w((La, MLA_KV_RANK, MLA_HEADS * (MLA_NOPE_DIM + MLA_V_DIM)), MLA_KV_RANK),
        "mla_w_o": w((La, MLA_HEADS * MLA_V_DIM, D), MLA_HEADS * MLA_V_DIM),
        "diff_w_qkv": w((Lb, D, 3 * D), D),
        "diff_lambda": 0.1 * jax.random.normal(next(ks), (Lb, 4, DIFF_HEAD_DIM), jnp.float32),
        "diff_subln": gains((Lb, 2 * DIFF_HEAD_DIM)),
        "diff_w_o": w((Lb, D, D), D),
        "ffn_w_gu": w((La, D, 2 * FFN_DIM), D),
        "ffn_w_down": w((La, FFN_DIM, D), FFN_DIM),
        "moe_router": w((Lb, D, N_EXPERTS), D),
        "moe_w_gu": w((Lb, N_EXPERTS, D, 2 * EXPERT_DIM), D),
        "moe_w_down": w((Lb, N_EXPERTS, EXPERT_DIM, D), EXPERT_DIM),
    }


def reference(x, c, ctx, c_ctx, ada_w, ada_b, norm_g,
              mla_w_dq, mla_q_norm, mla_w_uq, mla_w_dkv, mla_kv_norm, mla_w_ukv, mla_w_o,
              diff_w_qkv, diff_lambda, diff_subln, diff_w_o,
              ffn_w_gu, ffn_w_down, moe_router, moe_w_gu, moe_w_down):
    B, S, D = x.shape
    C = ctx.shape[1]
    rows = S // GRID_W
    row = jnp.repeat(jnp.arange(rows, dtype=jnp.float32), GRID_W)
    col = jnp.tile(jnp.arange(GRID_W, dtype=jnp.float32), rows)
    rope = functools.partial(axial_rope, row=row, col=col)

    silu_c = jax.nn.silu(c)
    silu_cc = jax.nn.silu(c_ctx)
    xl, xc = x, ctx
    for i in range(DEPTH):
        j = i // 2
        need_ctx = i < DEPTH - 1
        g = norm_g[i]
        mod_l = (silu_c @ ada_w[i] + ada_b[i]).reshape(B, N_MOD, 1, D)
        mod_c = (silu_cc @ ada_w[i] + ada_b[i]).reshape(N_MOD, D)

        h_l = modulate(xl, g[0], mod_l[:, 0], mod_l[:, 1])
        h_c = modulate(xc, g[0], mod_c[0], mod_c[1])
        if i % N_MIXERS == 0:
            y_l, y_c = mla_mixer(h_l, h_c, mla_w_dq[j], mla_q_norm[j], mla_w_uq[j], mla_w_dkv[j],
                                 mla_kv_norm[j], mla_w_ukv[j], mla_w_o[j], rope, need_ctx)
        else:
            lambda_init = 0.8 - 0.6 * math.exp(-0.3 * i)
            y_l, y_c = diff_mixer(h_l, h_c, diff_w_qkv[j], diff_lambda[j], diff_subln[j], diff_w_o[j],
                                  lambda_init, rope, need_ctx)
        xl = xl + mod_l[:, 2] * rms_norm(y_l, g[1])
        if need_ctx:
            xc = xc + mod_c[2] * rms_norm(y_c, g[1])

        h_l = modulate(xl, g[2], mod_l[:, 3], mod_l[:, 4])
        if need_ctx:
            h = jnp.concatenate([modulate(xc, g[2], mod_c[3], mod_c[4]), h_l], axis=1)
        else:
            h = h_l
        if i % 2 == 0:
            f = swiglu(h, ffn_w_gu[j], ffn_w_down[j])
        else:
            f = moe_ffn(h, moe_router[j], moe_w_gu[j], moe_w_down[j])
        xl = xl + mod_l[:, 5] * rms_norm(f[:, f.shape[1] - S:], g[3])
        if need_ctx:
            xc = xc + mod_c[5] * rms_norm(f[:, :C], g[3])
    return xl
```

```python
import functools
import math

import jax
import jax.numpy as jnp
from jax import lax
from jax.experimental import pallas as pl
from jax.experimental.pallas import tpu as pltpu

F32 = jnp.float32
BF16 = jnp.bfloat16

GRID_W = 64
ROPE_BASE = 10000.0
EPS = 1e-6
N_MOD = 6

MLA_HEADS = 32
MLA_Q_RANK = 1024
MLA_KV_RANK = 512
MLA_NOPE_DIM = 128
MLA_ROPE_DIM = 64
MLA_V_DIM = 128
MLA_QK_PAD = 256
MLA_SCALE = (MLA_NOPE_DIM + MLA_ROPE_DIM) ** -0.5

DIFF_HEAD_DIM = 128

N_EXPERTS = 8

LANES = 128
MOD_ROWS = 8
TOKEN_TILE = 768
VMEM_LIMIT = 56 * 1024 * 1024
LOG2E = math.log2(math.e)


def _params(*semantics):
    return pltpu.CompilerParams(dimension_semantics=semantics, vmem_limit_bytes=VMEM_LIMIT)


def _rms(x, g):
    return x * lax.rsqrt(jnp.mean(x * x, axis=-1, keepdims=True) + EPS) * g


def _swap_halves(x, half):
    lane = lax.broadcasted_iota(jnp.int32, x.shape, 1)
    first = (lane % (2 * half)) < half
    return jnp.where(first, pltpu.roll(x, LANES - half, 1), pltpu.roll(x, half, 1))


def _rope(x, cos, sin, half):
    return x * cos + _swap_halves(x, half) * sin


def _ada_kernel(c_ref, w_ref, b_ref, o_ref):
    c = c_ref[...]
    s = (c * jax.nn.sigmoid(c)).astype(BF16)
    o_ref[...] = jnp.dot(s, w_ref[...].astype(BF16), preferred_element_type=F32) + b_ref[...]


def _ada(cond, ada_w, ada_b3, layer):
    d = cond.shape[1]
    n = ada_w.shape[2]
    tn = 512
    return pl.pallas_call(
        _ada_kernel,
        grid=(n // tn,),
        in_specs=[
            pl.BlockSpec((MOD_ROWS, d), lambda j: (0, 0)),
            pl.BlockSpec((None, d, tn), lambda j: (layer, 0, j)),
            pl.BlockSpec((None, 1, tn), lambda j: (layer, 0, j)),
        ],
        out_specs=pl.BlockSpec((MOD_ROWS, tn), lambda j: (0, j)),
        out_shape=jax.ShapeDtypeStruct((MOD_ROWS, n), F32),
        compiler_params=_params("arbitrary"),
        name="ada",
    )(cond, ada_w, ada_b3)


def _stream_kernel(*refs, n_batch, d, resid, norm, router, gate_idx, shift_idx):
    refs = list(refs)
    x_ref = refs.pop(0)
    if resid:
        y_ref, gres_ref, mres_ref = refs.pop(0), refs.pop(0), refs.pop(0)
    if norm:
        gnorm_ref, mnorm_ref = refs.pop(0), refs.pop(0)
    if router:
        wr_ref = refs.pop(0)
    if resid:
        xo_ref = refs.pop(0)
    if norm:
        h_ref = refs.pop(0)
    if router:
        comb_ref = refs.pop(0)

    row = jnp.where(pl.program_id(1) == 0, n_batch, pl.program_id(0))

    def mod(ref, k):
        return ref[pl.ds(row, 1), pl.ds(k * d, d)]

    x = x_ref[...]
    if resid:
        x = x + mod(mres_ref, gate_idx) * _rms(y_ref[...].astype(F32), gres_ref[...])
        xo_ref[...] = x
    if norm:
        h = _rms(x, gnorm_ref[...]) * (1.0 + mod(mnorm_ref, shift_idx + 1)) + mod(mnorm_ref, shift_idx)
        h_ref[...] = h.astype(BF16)
    if router:
        logits = jnp.dot(h, wr_ref[...], precision=lax.Precision.HIGHEST, preferred_element_type=F32)
        lane = lax.broadcasted_iota(jnp.int32, logits.shape, 1)
        neg = jnp.float32(-jnp.inf)
        lg = jnp.where(lane < N_EXPERTS, logits, neg)
        m1 = jnp.max(lg, axis=1, keepdims=True)
        i1 = jnp.min(jnp.where(lg == m1, lane, LANES), axis=1, keepdims=True)
        lg2 = jnp.where(lane == i1, neg, lg)
        m2 = jnp.max(lg2, axis=1, keepdims=True)
        i2 = jnp.min(jnp.where(lg2 == m2, lane, LANES), axis=1, keepdims=True)
        e2 = jnp.exp(m2 - m1)
        w1 = 1.0 / (1.0 + e2)
        w2 = e2 / (1.0 + e2)
        comb_ref[...] = jnp.where(lane == i1, w1, jnp.where(lane == i2, w2, 0.0))


def _stream(x, *, n_batch, tile, y=None, g_res=None, mods_res=None, gate_idx=0,
            g_norm=None, mods_norm=None, shift_idx=0, w_router=None):
    n, d = x.shape
    resid, norm, router = y is not None, g_norm is not None, w_router is not None
    tpb = n // n_batch // tile
    row_spec = pl.BlockSpec((tile, d), lambda b, t: (b * tpb + t, 0))
    vec_spec = pl.BlockSpec((1, d), lambda b, t: (0, 0))
    mod_spec = pl.BlockSpec((MOD_ROWS, N_MOD * d), lambda b, t: (0, 0))
    args, in_specs, out_shape, out_specs = [x], [row_spec], [], []
    if resid:
        args += [y, g_res, mods_res]
        in_specs += [row_spec, vec_spec, mod_spec]
        out_shape.append(jax.ShapeDtypeStruct((n, d), F32))
        out_specs.append(row_spec)
    if norm:
        args += [g_norm, mods_norm]
        in_specs += [vec_spec, mod_spec]
        out_shape.append(jax.ShapeDtypeStruct((n, d), BF16))
        out_specs.append(row_spec)
    if router:
        args.append(w_router)
        in_specs.append(pl.BlockSpec((d, LANES), lambda b, t: (0, 0)))
        out_shape.append(jax.ShapeDtypeStruct((n, LANES), F32))
        out_specs.append(pl.BlockSpec((tile, LANES), lambda b, t: (b * tpb + t, 0)))
    kernel = functools.partial(_stream_kernel, n_batch=n_batch, d=d, resid=resid, norm=norm,
                               router=router, gate_idx=gate_idx, shift_idx=shift_idx)
    return pl.pallas_call(
        kernel,
        grid=(n_batch, tpb),
        in_specs=in_specs,
        out_specs=out_specs,
        out_shape=out_shape,
        input_output_aliases={0: 0} if resid else {},
        compiler_params=_params("arbitrary", "arbitrary"),
        name="stream",
    )(*args)


def _mm_kernel(x_ref, w_ref, o_ref):
    o_ref[...] = jnp.dot(x_ref[...], w_ref[...], preferred_element_type=F32).astype(o_ref.dtype)


def _matmul(x, w, *, tm, tn, out_dtype=BF16, name="mm"):
    m, k = x.shape
    n = w.shape[1]
    tn = min(tn, n)
    return pl.pallas_call(
        _mm_kernel,
        grid=(m // tm, n // tn),
        in_specs=[pl.BlockSpec((tm, k), lambda i, j: (i, 0)),
                  pl.BlockSpec((k, tn), lambda i, j: (0, j))],
        out_specs=pl.BlockSpec((tm, tn), lambda i, j: (i, j)),
        out_shape=jax.ShapeDtypeStruct((m, n), out_dtype),
        compiler_params=_params("arbitrary", "arbitrary"),
        name=name,
    )(x, w)


def _swiglu_kernel(*refs, tiles_per_expert):
    if tiles_per_expert:
        x_ref, wg_ref, wu_ref, comb_ref, o_ref = refs
    else:
        x_ref, wg_ref, wu_ref, o_ref = refs
    x = x_ref[...]
    g = jnp.dot(x, wg_ref[...], preferred_element_type=F32)
    u = jnp.dot(x, wu_ref[...], preferred_element_type=F32)
    a = g * jax.nn.sigmoid(g) * u
    if tiles_per_expert:
        e = pl.program_id(1) // tiles_per_expert
        comb = comb_ref[...]
        lane = lax.broadcasted_iota(jnp.int32, comb.shape, 1)
        a = a * jnp.sum(jnp.where(lane == e, comb, 0.0), axis=1, keepdims=True)
    o_ref[...] = a.astype(o_ref.dtype)


def _swiglu_dense(x, w_gu, *, tm, tn):
    m, k = x.shape
    f = w_gu.shape[1] // 2
    nj = f // tn
    return pl.pallas_call(
        functools.partial(_swiglu_kernel, tiles_per_expert=0),
        grid=(m // tm, nj),
        in_specs=[pl.BlockSpec((tm, k), lambda i, j: (i, 0)),
                  pl.BlockSpec((k, tn), lambda i, j: (0, j)),
                  pl.BlockSpec((k, tn), lambda i, j: (0, nj + j))],
        out_specs=pl.BlockSpec((tm, tn), lambda i, j: (i, j)),
        out_shape=jax.ShapeDtypeStruct((m, f), BF16),
        compiler_params=_params("arbitrary", "arbitrary"),
        name="swiglu",
    )(x, w_gu, w_gu)


def _swiglu_experts(x, w_gu, comb, *, tm, tn):
    m, k = x.shape
    n_exp, _, f2 = w_gu.shape
    f = f2 // 2
    tpe = f // tn
    return pl.pallas_call(
        functools.partial(_swiglu_kernel, tiles_per_expert=tpe),
        grid=(m // tm, n_exp * tpe),
        in_specs=[pl.BlockSpec((tm, k), lambda i, j: (i, 0)),
                  pl.BlockSpec((None, k, tn), lambda i, j: (j // tpe, 0, j % tpe)),
                  pl.BlockSpec((None, k, tn), lambda i, j: (j // tpe, 0, tpe + j % tpe)),
                  pl.BlockSpec((tm, LANES), lambda i, j: (i, 0))],
        out_specs=pl.BlockSpec((tm, tn), lambda i, j: (i, j)),
        out_shape=jax.ShapeDtypeStruct((m, n_exp * f), BF16),
        compiler_params=_params("arbitrary", "arbitrary"),
        name="swiglu_experts",
    )(x, w_gu, w_gu, comb)


def _rope_mm_kernel(x_ref, w_ref, cos_ref, sin_ref, o_ref, *, half, rope_every, scale):
    acc = jnp.dot(x_ref[...], w_ref[...], preferred_element_type=F32)
    cos, sin = cos_ref[...], sin_ref[...]
    for g in range(acc.shape[1] // LANES):
        blk = acc[:, g * LANES:(g + 1) * LANES]
        if g % rope_every == rope_every - 1:
            blk = _rope(blk, cos, sin, half)
        o_ref[:, g * LANES:(g + 1) * LANES] = (blk * scale).astype(o_ref.dtype)


def _rope_matmul(x, w, cos, sin, *, tm, tn, half, rope_every, scale, name):
    m, k = x.shape
    n = w.shape[1]
    tn = min(tn, n)
    tpb = cos.shape[0] // tm
    return pl.pallas_call(
        functools.partial(_rope_mm_kernel, half=half, rope_every=rope_every, scale=scale),
        grid=(m // tm, n // tn),
        in_specs=[pl.BlockSpec((tm, k), lambda i, j: (i, 0)),
                  pl.BlockSpec((k, tn), lambda i, j: (0, j)),
                  pl.BlockSpec((tm, LANES), lambda i, j: (i % tpb, 0)),
                  pl.BlockSpec((tm, LANES), lambda i, j: (i % tpb, 0))],
        out_specs=pl.BlockSpec((tm, tn), lambda i, j: (i, j)),
        out_shape=jax.ShapeDtypeStruct((m, n), BF16),
        compiler_params=_params("arbitrary", "arbitrary"),
        name=name,
    )(x, w, cos, sin)


def _vt_kernel(w_ref, x_ref, o_ref):
    o_ref[...] = lax.dot_general(w_ref[...], x_ref[...], (((1,), (1,)), ((), ())),
                                 preferred_element_type=F32).astype(o_ref.dtype)


def _vt_matmul(w_t, x, *, chunk, tv):
    dv, k = w_t.shape
    n = x.shape[0]
    tv = min(tv, dv)
    return pl.pallas_call(
        _vt_kernel,
        grid=(n // chunk, dv // tv),
        in_specs=[pl.BlockSpec((tv, k), lambda c, j: (j, 0)),
                  pl.BlockSpec((chunk, k), lambda c, j: (c, 0))],
        out_specs=pl.BlockSpec((None, tv, chunk), lambda c, j: (c, j, 0)),
        out_shape=jax.ShapeDtypeStruct((n // chunk, dv, chunk), BF16),
        compiler_params=_params("arbitrary", "arbitrary"),
        name="vt",
    )(w_t, x)


def _mla_down_kernel(x_ref, w_ref, qn_ref, kvn_ref, cos_ref, sin_ref, cq_ref, cn_ref, kpe_ref):
    acc = jnp.dot(x_ref[...], w_ref[...], preferred_element_type=F32)
    r0, r1 = MLA_Q_RANK, MLA_Q_RANK + MLA_KV_RANK
    cq_ref[...] = _rms(acc[:, :r0], qn_ref[...]).astype(BF16)
    cn_ref[...] = _rms(acc[:, r0:r1], kvn_ref[...]).astype(BF16)
    kpe_ref[...] = _rope(acc[:, r1:], cos_ref[...], sin_ref[...], MLA_ROPE_DIM // 4).astype(BF16)


def _mla_down(h, w_cat, q_norm, kv_norm, cos, sin, *, tm):
    m, k = h.shape
    n = w_cat.shape[1]
    tpb = cos.shape[0] // tm
    row = lambda width: pl.BlockSpec((tm, width), lambda i: (i, 0))
    tab = pl.BlockSpec((tm, LANES), lambda i: (i % tpb, 0))
    return pl.pallas_call(
        _mla_down_kernel,
        grid=(m // tm,),
        in_specs=[row(k),
                  pl.BlockSpec((k, n), lambda i: (0, 0)),
                  pl.BlockSpec((1, MLA_Q_RANK), lambda i: (0, 0)),
                  pl.BlockSpec((1, MLA_KV_RANK), lambda i: (0, 0)),
                  tab, tab],
        out_specs=[row(MLA_Q_RANK), row(MLA_KV_RANK), row(LANES)],
        out_shape=[jax.ShapeDtypeStruct((m, MLA_Q_RANK), BF16),
                   jax.ShapeDtypeStruct((m, MLA_KV_RANK), BF16),
                   jax.ShapeDtypeStruct((m, LANES), BF16)],
        compiler_params=_params("arbitrary"),
        name="mla_down",
    )(h, w_cat, q_norm, kv_norm, cos, sin)


def _mla_kup_kernel(c_ref, w_ref, kpe_ref, o_ref):
    acc = jnp.dot(c_ref[...], w_ref[...], preferred_element_type=F32).astype(BF16)
    kpe = kpe_ref[...]
    for h in range(MLA_HEADS):
        o_ref[:, h * MLA_QK_PAD:h * MLA_QK_PAD + LANES] = acc[:, h * LANES:(h + 1) * LANES]
        o_ref[:, h * MLA_QK_PAD + LANES:(h + 1) * MLA_QK_PAD] = kpe


def _mla_kup(cn, w_uk, kpe, *, tm):
    m, k = cn.shape
    n_out = MLA_HEADS * MLA_QK_PAD
    return pl.pallas_call(
        _mla_kup_kernel,
        grid=(m // tm,),
        in_specs=[pl.BlockSpec((tm, k), lambda i: (i, 0)),
                  pl.BlockSpec((k, MLA_HEADS * MLA_NOPE_DIM), lambda i: (0, 0)),
                  pl.BlockSpec((tm, LANES), lambda i: (i, 0))],
        out_specs=pl.BlockSpec((tm, n_out), lambda i: (i, 0)),
        out_shape=jax.ShapeDtypeStruct((m, n_out), BF16),
        compiler_params=_params("arbitrary"),
        name="mla_kup",
    )(cn, w_uk, kpe)


def _softmax_step(s, v_t, carry):
    m, l, acc = carry
    m_new = jnp.maximum(m, jnp.max(s, axis=0, keepdims=True))
    alpha = jnp.exp2(m - m_new)
    p = jnp.exp2(s - m_new)
    l = alpha * l + jnp.sum(p, axis=0, keepdims=True)
    acc = alpha * acc + jnp.dot(v_t, p.astype(BF16), preferred_element_type=F32)
    return m_new, l, acc


def _scores(k, q):
    return lax.dot_general(k, q, (((1,), (1,)), ((), ())), preferred_element_type=F32)


def _softmax_init(dv, tq):
    return (jnp.full((1, tq), -jnp.inf, F32), jnp.zeros((1, tq), F32), jnp.zeros((dv, tq), F32))


def _mla_attn_kernel(q_ref, k_ref, vt_ref, o_ref, *, n_chunks, chunk, ctx_len):
    q = q_ref[...]
    tq = q.shape[0]
    init = _softmax_init(MLA_V_DIM, tq)

    def finish(carry):
        _, l, acc = carry
        o_ref[...] = (acc / l).T.astype(o_ref.dtype)

    @pl.when(pl.program_id(2) == 0)
    def _():
        finish(_softmax_step(_scores(k_ref[0:ctx_len, :], q), vt_ref[0, :, 0:ctx_len], init))

    @pl.when(pl.program_id(2) > 0)
    def _():
        def body(c, carry):
            k = k_ref[pl.ds(pl.multiple_of(c * chunk, chunk), chunk), :]
            return _softmax_step(_scores(k, q), vt_ref[c], carry)
        finish(lax.fori_loop(0, n_chunks, body, init))


def _diff_attn_kernel(lam_ref, sub_ref, q_ref, k_ref, vt_ref, o_ref, *, n_chunks, chunk, ctx_len,
                      lambda_init):
    q = q_ref[...]
    tq = q.shape[0]
    d = DIFF_HEAD_DIM
    q1, q2 = q[:, :d], q[:, d:]
    init = _softmax_init(2 * d, tq)

    def step(k, v_t, carry):
        c1, c2 = carry
        return (_softmax_step(_scores(k[:, :d], q1), v_t, c1),
                _softmax_step(_scores(k[:, d:], q2), v_t, c2))

    def finish(carry):
        (_, l1, a1), (_, l2, a2) = carry
        lp = lam_ref[...]
        lam = (jnp.exp(jnp.sum(lp[0:1] * lp[1:2], keepdims=True))
               - jnp.exp(jnp.sum(lp[2:3] * lp[3:4], keepdims=True)) + lambda_init)
        o = (a1 / l1 - lam * (a2 / l2)).T
        o_ref[...] = (_rms(o, sub_ref[...]) * (1.0 - lambda_init)).astype(o_ref.dtype)

    @pl.when(pl.program_id(2) == 0)
    def _():
        finish(step(k_ref[0:ctx_len, :], vt_ref[0, :, 0:ctx_len], (init, init)))

    @pl.when(pl.program_id(2) > 0)
    def _():
        def body(c, carry):
            k = k_ref[pl.ds(pl.multiple_of(c * chunk, chunk), chunk), :]
            return step(k, vt_ref[c], carry)
        finish(lax.fori_loop(0, n_chunks, body, (init, init)))


def _attention(kernel, q, k, vt, *, n_batch, n_heads, dqk, dv, tq, chunk, extra=(), name):
    n = q.shape[0]
    tokens = n // n_batch
    nq = tokens // tq
    n_chunks = tokens // chunk
    extra_specs = [pl.BlockSpec(e.shape, lambda b, h, i: (0, 0)) for e in extra]
    return pl.pallas_call(
        functools.partial(kernel, n_chunks=n_chunks, chunk=chunk, ctx_len=tq),
        grid=(n_batch, n_heads, nq),
        in_specs=extra_specs + [
            pl.BlockSpec((tq, dqk), lambda b, h, i: (b * nq + i, h)),
            pl.BlockSpec((tokens, dqk), lambda b, h, i: (b, h)),
            pl.BlockSpec((n_chunks, dv, chunk), lambda b, h, i: (b, h, 0)),
        ],
        out_specs=pl.BlockSpec((tq, dv), lambda b, h, i: (b * nq + i, h)),
        out_shape=jax.ShapeDtypeStruct((n, n_heads * dv), BF16),
        compiler_params=_params("arbitrary", "arbitrary", "arbitrary"),
        name=name,
    )(*extra, q, k, vt)


def _rope_tables(ctx_len, seq, rot_dim):
    half = rot_dim // 2
    n_freq = half // 2
    t = jnp.arange(seq, dtype=jnp.int32)
    pos_row = (t // GRID_W).astype(F32)
    pos_col = (t % GRID_W).astype(F32)
    inv = ROPE_BASE ** (-jnp.arange(0, half, 2, dtype=F32) / half)
    lane = jnp.arange(LANES)
    freq = inv[(lane % half) % n_freq]
    pos = jnp.where((lane // half)[None, :] == 0, pos_row[:, None], pos_col[:, None])
    ang = pos * freq[None, :]
    live = (lane < rot_dim)[None, :]
    sign = jnp.where((lane % half) < n_freq, -1.0, 1.0)[None, :]
    cos = jnp.where(live, jnp.cos(ang), 1.0)
    sin = jnp.where(live, jnp.sin(ang) * sign, 0.0)
    cos = jnp.concatenate([jnp.ones((ctx_len, LANES), F32), cos], axis=0)
    sin = jnp.concatenate([jnp.zeros((ctx_len, LANES), F32), sin], axis=0)
    return cos, sin


def kernel(x, c, ctx, c_ctx, ada_w, ada_b, norm_g, mla_w_dq, mla_q_norm, mla_w_uq, mla_w_dkv, mla_kv_norm, mla_w_ukv, mla_w_o, diff_w_qkv, diff_lambda, diff_subln, diff_w_o, ffn_w_gu, ffn_w_down, moe_router, moe_w_gu, moe_w_down):
    n_batch, seq, d = x.shape
    ctx_len = ctx.shape[1]
    depth = ada_w.shape[0]
    tokens = ctx_len + seq
    tm = TOKEN_TILE
    assert tokens % tm == 0 and tokens % ctx_len == 0 and ctx_len % LANES == 0 and ctx_len <= tm
    assert n_batch < MOD_ROWS
    diff_heads = d // (2 * DIFF_HEAD_DIM)

    xs = jnp.concatenate([ctx, x], axis=1).reshape(n_batch * tokens, d)
    cond = jnp.zeros((MOD_ROWS, d), F32).at[:n_batch].set(c).at[n_batch].set(c_ctx)
    ada_b3 = ada_b.reshape(depth, 1, N_MOD * d)
    stream = functools.partial(_stream, n_batch=n_batch, tile=ctx_len)

    cos_d, sin_d = _rope_tables(ctx_len, seq, DIFF_HEAD_DIM)
    cos_m, sin_m = _rope_tables(ctx_len, seq, MLA_ROPE_DIM)

    mods = _ada(cond, ada_w, ada_b3, 0)
    (h,) = stream(xs, g_norm=norm_g[0, 0:1], mods_norm=mods, shift_idx=0)

    for i in range(depth):
        j = i // 2
        g = norm_g[i]
        if i % 2 == 0:
            w_cat = jnp.concatenate(
                [mla_w_dq[j], mla_w_dkv[j], jnp.zeros((d, LANES - MLA_ROPE_DIM), F32)], axis=1).astype(BF16)
            w_uq = mla_w_uq[j].reshape(MLA_Q_RANK, MLA_HEADS, MLA_NOPE_DIM + MLA_ROPE_DIM)
            w_uq = jnp.pad(w_uq, ((0, 0), (0, 0), (0, MLA_QK_PAD - w_uq.shape[2])))
            w_uq = w_uq.reshape(MLA_Q_RANK, MLA_HEADS * MLA_QK_PAD).astype(BF16)
            w_ukv = mla_w_ukv[j].reshape(MLA_KV_RANK, MLA_HEADS, MLA_NOPE_DIM + MLA_V_DIM)
            w_uk = w_ukv[:, :, :MLA_NOPE_DIM].reshape(MLA_KV_RANK, -1).astype(BF16)
            w_uv_t = w_ukv[:, :, MLA_NOPE_DIM:].reshape(MLA_KV_RANK, -1).T.astype(BF16)

            cq, cn, kpe = _mla_down(h, w_cat, mla_q_norm[j][None], mla_kv_norm[j][None], cos_m, sin_m, tm=tm)
            q = _rope_matmul(cq, w_uq, cos_m, sin_m, tm=tm, tn=2048, half=MLA_ROPE_DIM // 4,
                             rope_every=2, scale=MLA_SCALE * LOG2E, name="mla_q")
            k = _mla_kup(cn, w_uk, kpe, tm=tm)
            vt = _vt_matmul(w_uv_t, cn, chunk=tm, tv=w_uv_t.shape[0])
            o = _attention(_mla_attn_kernel, q, k, vt, n_batch=n_batch, n_heads=MLA_HEADS,
                           dqk=MLA_QK_PAD, dv=MLA_V_DIM, tq=ctx_len, chunk=tm, name="mla_attn")
            y = _matmul(o, mla_w_o[j].astype(BF16), tm=tm, tn=1024, name="mla_o")
        else:
            lambda_init = 0.8 - 0.6 * math.exp(-0.3 * i)
            w_q = diff_w_qkv[j][:, :d].astype(BF16)
            w_k = diff_w_qkv[j][:, d:2 * d].astype(BF16)
            w_v_t = diff_w_qkv[j][:, 2 * d:].T.astype(BF16)
            scale = DIFF_HEAD_DIM ** -0.5 * LOG2E
            q = _rope_matmul(h, w_q, cos_d, sin_d, tm=tm, tn=1024, half=DIFF_HEAD_DIM // 4,
                             rope_every=1, scale=scale, name="diff_q")
            k = _rope_matmul(h, w_k, cos_d, sin_d, tm=tm, tn=1024, half=DIFF_HEAD_DIM // 4,
                             rope_every=1, scale=1.0, name="diff_k")
            vt = _vt_matmul(w_v_t, h, chunk=tm, tv=1024)
            o = _attention(functools.partial(_diff_attn_kernel, lambda_init=lambda_init), q, k, vt,
                           n_batch=n_batch, n_heads=diff_heads, dqk=2 * DIFF_HEAD_DIM,
                           dv=2 * DIFF_HEAD_DIM, tq=ctx_len, chunk=tm,
                           extra=(diff_lambda[j], diff_subln[j][None]), name="diff_attn")
            y = _matmul(o, diff_w_o[j].astype(BF16), tm=tm, tn=1024, name="diff_o")

        if i % 2 == 0:
            xs, h = stream(xs, y=y, g_res=g[1:2], mods_res=mods, gate_idx=2,
                           g_norm=g[2:3], mods_norm=mods, shift_idx=3)
            a = _swiglu_dense(h, ffn_w_gu[j].astype(BF16), tm=tm, tn=512)
            f = _matmul(a, ffn_w_down[j].astype(BF16), tm=tm, tn=1024, name="ffn_down")
        else:
            w_r = jnp.pad(moe_router[j], ((0, 0), (0, LANES - N_EXPERTS)))
            xs, h, comb = stream(xs, y=y, g_res=g[1:2], mods_res=mods, gate_idx=2,
                                 g_norm=g[2:3], mods_norm=mods, shift_idx=3, w_router=w_r)
            a = _swiglu_experts(h, moe_w_gu[j].astype(BF16), comb, tm=tm, tn=512)
            w_down = moe_w_down[j].reshape(-1, d).astype(BF16)
            f = _matmul(a, w_down, tm=tm, tn=512, name="moe_down")

        if i + 1 < depth:
            mods_next = _ada(cond, ada_w, ada_b3, i + 1)
            xs, h = stream(xs, y=f, g_res=g[3:4], mods_res=mods, gate_idx=5,
                           g_norm=norm_g[i + 1, 0:1], mods_norm=mods_next, shift_idx=0)
            mods = mods_next
        else:
            (xs,) = stream(xs, y=f, g_res=g[3:4], mods_res=mods, gate_idx=5)

    return xs.reshape(n_batch, tokens, d)[:, ctx_len:]
```

```python
import functools
import math

import jax
import jax.numpy as jnp
from jax import lax
from jax.experimental import pallas as pl
from jax.experimental.pallas import tpu as pltpu

F32 = jnp.float32
BF16 = jnp.bfloat16

GRID_W = 64
ROPE_BASE = 10000.0
EPS = 1e-6
N_MOD = 6

MLA_HEADS = 32
MLA_Q_RANK = 1024
MLA_KV_RANK = 512
MLA_NOPE_DIM = 128
MLA_ROPE_DIM = 64
MLA_V_DIM = 128
MLA_QK_PAD = 256
MLA_SCALE = (MLA_NOPE_DIM + MLA_ROPE_DIM) ** -0.5

DIFF_HEAD_DIM = 128

N_EXPERTS = 8

LANES = 128
MOD_ROWS = 8
TOKEN_TILE = 768
ATTN_Q_TILE = 512
ATTN_STREAMS = 2
VMEM_LIMIT = 56 * 1024 * 1024
LOG2E = math.log2(math.e)


def _params(*semantics):
    return pltpu.CompilerParams(dimension_semantics=semantics, vmem_limit_bytes=VMEM_LIMIT)


def _rms(x, g):
    return x * lax.rsqrt(jnp.mean(x * x, axis=-1, keepdims=True) + EPS) * g


def _swap_halves(x, half):
    lane = lax.broadcasted_iota(jnp.int32, x.shape, 1)
    first = (lane % (2 * half)) < half
    return jnp.where(first, pltpu.roll(x, LANES - half, 1), pltpu.roll(x, half, 1))


def _rope(x, cos, sin, half):
    return x * cos + _swap_halves(x, half) * sin


def _ada_kernel(c_ref, w_ref, b_ref, o_ref):
    c = c_ref[...]
    s = (c * jax.nn.sigmoid(c)).astype(BF16)
    o_ref[...] = jnp.dot(s, w_ref[...].astype(BF16), preferred_element_type=F32) + b_ref[...]


def _ada(cond, ada_w, ada_b3, layer):
    d = cond.shape[1]
    n = ada_w.shape[2]
    tn = 512
    return pl.pallas_call(
        _ada_kernel,
        grid=(n // tn,),
        in_specs=[
            pl.BlockSpec((MOD_ROWS, d), lambda j: (0, 0)),
            pl.BlockSpec((None, d, tn), lambda j: (layer, 0, j)),
            pl.BlockSpec((None, 1, tn), lambda j: (layer, 0, j)),
        ],
        out_specs=pl.BlockSpec((MOD_ROWS, tn), lambda j: (0, j)),
        out_shape=jax.ShapeDtypeStruct((MOD_ROWS, n), F32),
        compiler_params=_params("arbitrary"),
        name="ada",
    )(cond, ada_w, ada_b3)


def _stream_kernel(*refs, n_batch, d, resid, norm, router, gate_idx, shift_idx):
    refs = list(refs)
    x_ref = refs.pop(0)
    if resid:
        y_ref, gres_ref, mres_ref = refs.pop(0), refs.pop(0), refs.pop(0)
    if norm:
        gnorm_ref, mnorm_ref = refs.pop(0), refs.pop(0)
    if router:
        wr_ref = refs.pop(0)
    if resid:
        xo_ref = refs.pop(0)
    if norm:
        h_ref = refs.pop(0)
    if router:
        comb_ref = refs.pop(0)

    row = jnp.where(pl.program_id(1) == 0, n_batch, pl.program_id(0))

    def mod(ref, k):
        return ref[pl.ds(row, 1), pl.ds(k * d, d)]

    x = x_ref[...]
    if resid:
        x = x + mod(mres_ref, gate_idx) * _rms(y_ref[...].astype(F32), gres_ref[...])
        xo_ref[...] = x
    if norm:
        h = _rms(x, gnorm_ref[...]) * (1.0 + mod(mnorm_ref, shift_idx + 1)) + mod(mnorm_ref, shift_idx)
        h_ref[...] = h.astype(BF16)
    if router:
        logits = jnp.dot(h, wr_ref[...], precision=lax.Precision.HIGHEST, preferred_element_type=F32)
        lane = lax.broadcasted_iota(jnp.int32, logits.shape, 1)
        neg = jnp.float32(-jnp.inf)
        lg = jnp.where(lane < N_EXPERTS, logits, neg)
        m1 = jnp.max(lg, axis=1, keepdims=True)
        i1 = jnp.min(jnp.where(lg == m1, lane, LANES), axis=1, keepdims=True)
        lg2 = jnp.where(lane == i1, neg, lg)
        m2 = jnp.max(lg2, axis=1, keepdims=True)
        i2 = jnp.min(jnp.where(lg2 == m2, lane, LANES), axis=1, keepdims=True)
        e2 = jnp.exp(m2 - m1)
        w1 = 1.0 / (1.0 + e2)
        w2 = e2 / (1.0 + e2)
        comb_ref[...] = jnp.where(lane == i1, w1, jnp.where(lane == i2, w2, 0.0))


def _stream(x, *, n_batch, tile, y=None, g_res=None, mods_res=None, gate_idx=0,
            g_norm=None, mods_norm=None, shift_idx=0, w_router=None):
    n, d = x.shape
    resid, norm, router = y is not None, g_norm is not None, w_router is not None
    tpb = n // n_batch // tile
    row_spec = pl.BlockSpec((tile, d), lambda b, t: (b * tpb + t, 0))
    vec_spec = pl.BlockSpec((1, d), lambda b, t: (0, 0))
    mod_spec = pl.BlockSpec((MOD_ROWS, N_MOD * d), lambda b, t: (0, 0))
    args, in_specs, out_shape, out_specs = [x], [row_spec], [], []
    if resid:
        args += [y, g_res, mods_res]
        in_specs += [row_spec, vec_spec, mod_spec]
        out_shape.append(jax.ShapeDtypeStruct((n, d), F32))
        out_specs.append(row_spec)
    if norm:
        args += [g_norm, mods_norm]
        in_specs += [vec_spec, mod_spec]
        out_shape.append(jax.ShapeDtypeStruct((n, d), BF16))
        out_specs.append(row_spec)
    if router:
        args.append(w_router)
        in_specs.append(pl.BlockSpec((d, LANES), lambda b, t: (0, 0)))
        out_shape.append(jax.ShapeDtypeStruct((n, LANES), F32))
        out_specs.append(pl.BlockSpec((tile, LANES), lambda b, t: (b * tpb + t, 0)))
    kernel = functools.partial(_stream_kernel, n_batch=n_batch, d=d, resid=resid, norm=norm,
                               router=router, gate_idx=gate_idx, shift_idx=shift_idx)
    return pl.pallas_call(
        kernel,
        grid=(n_batch, tpb),
        in_specs=in_specs,
        out_specs=out_specs,
        out_shape=out_shape,
        input_output_aliases={0: 0} if resid else {},
        compiler_params=_params("arbitrary", "arbitrary"),
        name="stream",
    )(*args)


def _mm_kernel(x_ref, w_ref, o_ref):
    o_ref[...] = jnp.dot(x_ref[...], w_ref[...], preferred_element_type=F32).astype(o_ref.dtype)


def _matmul(x, w, *, tm, tn, out_dtype=BF16, name="mm"):
    m, k = x.shape
    n = w.shape[1]
    tn = min(tn, n)
    return pl.pallas_call(
        _mm_kernel,
        grid=(m // tm, n // tn),
        in_specs=[pl.BlockSpec((tm, k), lambda i, j: (i, 0)),
                  pl.BlockSpec((k, tn), lambda i, j: (0, j))],
        out_specs=pl.BlockSpec((tm, tn), lambda i, j: (i, j)),
        out_shape=jax.ShapeDtypeStruct((m, n), out_dtype),
        compiler_params=_params("arbitrary", "arbitrary"),
        name=name,
    )(x, w)


def _swiglu_kernel(*refs, tiles_per_expert):
    if tiles_per_expert:
        x_ref, wg_ref, wu_ref, comb_ref, o_ref = refs
    else:
        x_ref, wg_ref, wu_ref, o_ref = refs
    x = x_ref[...]
    g = jnp.dot(x, wg_ref[...], preferred_element_type=F32)
    u = jnp.dot(x, wu_ref[...], preferred_element_type=F32)
    a = g * jax.nn.sigmoid(g) * u
    if tiles_per_expert:
        e = pl.program_id(1) // tiles_per_expert
        comb = comb_ref[...]
        lane = lax.broadcasted_iota(jnp.int32, comb.shape, 1)
        a = a * jnp.sum(jnp.where(lane == e, comb, 0.0), axis=1, keepdims=True)
    o_ref[...] = a.astype(o_ref.dtype)


def _swiglu_dense(x, w_gu, *, tm, tn):
    m, k = x.shape
    f = w_gu.shape[1] // 2
    nj = f // tn
    return pl.pallas_call(
        functools.partial(_swiglu_kernel, tiles_per_expert=0),
        grid=(m // tm, nj),
        in_specs=[pl.BlockSpec((tm, k), lambda i, j: (i, 0)),
                  pl.BlockSpec((k, tn), lambda i, j: (0, j)),
                  pl.BlockSpec((k, tn), lambda i, j: (0, nj + j))],
        out_specs=pl.BlockSpec((tm, tn), lambda i, j: (i, j)),
        out_shape=jax.ShapeDtypeStruct((m, f), BF16),
        compiler_params=_params("arbitrary", "arbitrary"),
        name="swiglu",
    )(x, w_gu, w_gu)


def _swiglu_experts(x, w_gu, comb, *, tm, tn):
    m, k = x.shape
    n_exp, _, f2 = w_gu.shape
    f = f2 // 2
    tpe = f // tn
    return pl.pallas_call(
        functools.partial(_swiglu_kernel, tiles_per_expert=tpe),
        grid=(m // tm, n_exp * tpe),
        in_specs=[pl.BlockSpec((tm, k), lambda i, j: (i, 0)),
                  pl.BlockSpec((None, k, tn), lambda i, j: (j // tpe, 0, j % tpe)),
                  pl.BlockSpec((None, k, tn), lambda i, j: (j // tpe, 0, tpe + j % tpe)),
                  pl.BlockSpec((tm, LANES), lambda i, j: (i, 0))],
        out_specs=pl.BlockSpec((tm, tn), lambda i, j: (i, j)),
        out_shape=jax.ShapeDtypeStruct((m, n_exp * f), BF16),
        compiler_params=_params("arbitrary", "arbitrary"),
        name="swiglu_experts",
    )(x, w_gu, w_gu, comb)


def _rope_mm_kernel(x_ref, w_ref, cos_ref, sin_ref, o_ref, *, half, rope_every, scale):
    acc = jnp.dot(x_ref[...], w_ref[...], preferred_element_type=F32)
    cos, sin = cos_ref[...], sin_ref[...]
    for g in range(acc.shape[1] // LANES):
        blk = acc[:, g * LANES:(g + 1) * LANES]
        if g % rope_every == rope_every - 1:
            blk = _rope(blk, cos, sin, half)
        o_ref[:, g * LANES:(g + 1) * LANES] = (blk * scale).astype(o_ref.dtype)


def _rope_matmul(x, w, cos, sin, *, tm, tn, half, rope_every, scale, name):
    m, k = x.shape
    n = w.shape[1]
    tn = min(tn, n)
    tpb = cos.shape[0] // tm
    return pl.pallas_call(
        functools.partial(_rope_mm_kernel, half=half, rope_every=rope_every, scale=scale),
        grid=(m // tm, n // tn),
        in_specs=[pl.BlockSpec((tm, k), lambda i, j: (i, 0)),
                  pl.BlockSpec((k, tn), lambda i, j: (0, j)),
                  pl.BlockSpec((tm, LANES), lambda i, j: (i % tpb, 0)),
                  pl.BlockSpec((tm, LANES), lambda i, j: (i % tpb, 0))],
        out_specs=pl.BlockSpec((tm, tn), lambda i, j: (i, j)),
        out_shape=jax.ShapeDtypeStruct((m, n), BF16),
        compiler_params=_params("arbitrary", "arbitrary"),
        name=name,
    )(x, w, cos, sin)


def _vt_kernel(w_ref, x_ref, o_ref):
    o_ref[...] = lax.dot_general(w_ref[...], x_ref[...], (((1,), (1,)), ((), ())),
                                 preferred_element_type=F32).astype(o_ref.dtype)


def _vt_matmul(w_t, x, *, chunk, tv):
    dv, k = w_t.shape
    n = x.shape[0]
    tv = min(tv, dv)
    return pl.pallas_call(
        _vt_kernel,
        grid=(n // chunk, dv // tv),
        in_specs=[pl.BlockSpec((tv, k), lambda c, j: (j, 0)),
                  pl.BlockSpec((chunk, k), lambda c, j: (c, 0))],
        out_specs=pl.BlockSpec((None, tv, chunk), lambda c, j: (c, j, 0)),
        out_shape=jax.ShapeDtypeStruct((n // chunk, dv, chunk), BF16),
        compiler_params=_params("arbitrary", "arbitrary"),
        name="vt",
    )(w_t, x)


def _mla_down_kernel(x_ref, w_ref, qn_ref, kvn_ref, cos_ref, sin_ref, cq_ref, cn_ref, kpe_ref):
    acc = jnp.dot(x_ref[...], w_ref[...], preferred_element_type=F32)
    r0, r1 = MLA_Q_RANK, MLA_Q_RANK + MLA_KV_RANK
    cq_ref[...] = _rms(acc[:, :r0], qn_ref[...]).astype(BF16)
    cn_ref[...] = _rms(acc[:, r0:r1], kvn_ref[...]).astype(BF16)
    kpe_ref[...] = _rope(acc[:, r1:], cos_ref[...], sin_ref[...], MLA_ROPE_DIM // 4).astype(BF16)


def _mla_down(h, w_cat, q_norm, kv_norm, cos, sin, *, tm):
    m, k = h.shape
    n = w_cat.shape[1]
    tpb = cos.shape[0] // tm
    row = lambda width: pl.BlockSpec((tm, width), lambda i: (i, 0))
    tab = pl.BlockSpec((tm, LANES), lambda i: (i % tpb, 0))
    return pl.pallas_call(
        _mla_down_kernel,
        grid=(m // tm,),
        in_specs=[row(k),
                  pl.BlockSpec((k, n), lambda i: (0, 0)),
                  pl.BlockSpec((1, MLA_Q_RANK), lambda i: (0, 0)),
                  pl.BlockSpec((1, MLA_KV_RANK), lambda i: (0, 0)),
                  tab, tab],
        out_specs=[row(MLA_Q_RANK), row(MLA_KV_RANK), row(LANES)],
        out_shape=[jax.ShapeDtypeStruct((m, MLA_Q_RANK), BF16),
                   jax.ShapeDtypeStruct((m, MLA_KV_RANK), BF16),
                   jax.ShapeDtypeStruct((m, LANES), BF16)],
        compiler_params=_params("arbitrary"),
        name="mla_down",
    )(h, w_cat, q_norm, kv_norm, cos, sin)


def _mla_kup_kernel(c_ref, w_ref, kpe_ref, o_ref):
    acc = jnp.dot(c_ref[...], w_ref[...], preferred_element_type=F32).astype(BF16)
    kpe = kpe_ref[...]
    for h in range(MLA_HEADS):
        o_ref[:, h * MLA_QK_PAD:h * MLA_QK_PAD + LANES] = acc[:, h * LANES:(h + 1) * LANES]
        o_ref[:, h * MLA_QK_PAD + LANES:(h + 1) * MLA_QK_PAD] = kpe


def _mla_kup(cn, w_uk, kpe, *, tm):
    m, k = cn.shape
    n_out = MLA_HEADS * MLA_QK_PAD
    return pl.pallas_call(
        _mla_kup_kernel,
        grid=(m // tm,),
        in_specs=[pl.BlockSpec((tm, k), lambda i: (i, 0)),
                  pl.BlockSpec((k, MLA_HEADS * MLA_NOPE_DIM), lambda i: (0, 0)),
                  pl.BlockSpec((tm, LANES), lambda i: (i, 0))],
        out_specs=pl.BlockSpec((tm, n_out), lambda i: (i, 0)),
        out_shape=jax.ShapeDtypeStruct((m, n_out), BF16),
        compiler_params=_params("arbitrary"),
        name="mla_kup",
    )(cn, w_uk, kpe)


def _fold_rows(op, s):
    r = s.shape[0]
    while r > 8 and r % 16 == 0:
        r //= 2
        s = op(s[:r], s[r:])
    return s


def _scores(k, q):
    return lax.dot_general(k, q, (((1,), (1,)), ((), ())), preferred_element_type=F32)


def _attend(q_ref, k_ref, vt_ref, o_ref, scratch, finish, *, n_maps, n_chunks, chunk, ctx_len, tq):
    s0_ref, s1_ref, acc_ref = scratch
    tokens, dqk = q_ref.shape
    dm = dqk // n_maps
    n_lanes = ATTN_STREAMS * n_maps

    def softmax_update(s, m, l):
        m_new = jnp.maximum(m, jnp.max(_fold_rows(jnp.maximum, s), axis=0, keepdims=True))
        alpha = jnp.exp2(m - m_new)
        p = jnp.exp2(s - m_new)
        l = alpha * l + jnp.sum(_fold_rows(jnp.add, p), axis=0, keepdims=True)
        return m_new, l, alpha, p.astype(BF16)

    qc, kc, vc = q_ref[0:ctx_len, :], k_ref[0:ctx_len, :], vt_ref[0, :, 0:ctx_len]
    parts = []
    for a in range(n_maps):
        s = _scores(kc[:, a * dm:(a + 1) * dm], qc[:, a * dm:(a + 1) * dm])
        _, l, _, p = softmax_update(s, jnp.full((1, ctx_len), -jnp.inf, F32), jnp.zeros((1, ctx_len), F32))
        parts.append((l, jnp.dot(vc, p, preferred_element_type=F32)))
    o_ref[0:ctx_len, :] = finish(parts)

    def q_body(i, _):
        offs = [pl.multiple_of(ctx_len + (i * ATTN_STREAMS + j) * tq, math.gcd(ctx_len, tq))
                for j in range(ATTN_STREAMS)]
        qs = [q_ref[pl.ds(off, tq), :] for off in offs]

        def put_scores(c, s_ref):
            k = k_ref[pl.ds(pl.multiple_of(c * chunk, chunk), chunk), :]
            for j, q in enumerate(qs):
                for a in range(n_maps):
                    s_ref[j * n_maps + a] = _scores(k[:, a * dm:(a + 1) * dm], q[:, a * dm:(a + 1) * dm])

        def consume(c, s_ref, ml):
            v_t = vt_ref[c]
            out = []
            for w in range(n_lanes):
                m, l, alpha, p = softmax_update(s_ref[w], *ml[w])
                acc_ref[w] = alpha * acc_ref[w] + jnp.dot(v_t, p, preferred_element_type=F32)
                out.append((m, l))
            return tuple(out)

        def kv_body(cc, ml):
            c = 2 * cc
            put_scores(c + 1, s1_ref)
            ml = consume(c, s0_ref, ml)
            put_scores(c + 2, s0_ref)
            return consume(c + 1, s1_ref, ml)

        put_scores(0, s0_ref)
        acc_ref[...] = jnp.zeros(acc_ref.shape, F32)
        ml = tuple((jnp.full((1, tq), -jnp.inf, F32), jnp.zeros((1, tq), F32)) for _ in range(n_lanes))
        n_pairs = (n_chunks - 1) // 2
        ml = lax.fori_loop(0, n_pairs, kv_body, ml)
        if n_chunks - 2 * n_pairs == 2:
            put_scores(n_chunks - 1, s1_ref)
            ml = consume(n_chunks - 2, s0_ref, ml)
            ml = consume(n_chunks - 1, s1_ref, ml)
        else:
            ml = consume(n_chunks - 1, s0_ref, ml)
        for j, off in enumerate(offs):
            o_ref[pl.ds(off, tq), :] = finish(
                [(ml[j * n_maps + a][1], acc_ref[j * n_maps + a]) for a in range(n_maps)])
        return 0

    lax.fori_loop(0, (tokens - ctx_len) // (tq * ATTN_STREAMS), q_body, 0)


def _mla_attn_kernel(q_ref, k_ref, vt_ref, o_ref, *scratch, **kw):
    def finish(parts):
        ((l, acc),) = parts
        return (acc / l).T.astype(o_ref.dtype)

    _attend(q_ref, k_ref, vt_ref, o_ref, scratch, finish, n_maps=1, **kw)


def _diff_attn_kernel(lam_ref, sub_ref, q_ref, k_ref, vt_ref, o_ref, *scratch, lambda_init, **kw):
    def finish(parts):
        (l1, a1), (l2, a2) = parts
        lp = lam_ref[...]
        lam = (jnp.exp(jnp.sum(lp[0:1] * lp[1:2], keepdims=True))
               - jnp.exp(jnp.sum(lp[2:3] * lp[3:4], keepdims=True)) + lambda_init)
        o = (a1 / l1 - lam * (a2 / l2)).T
        return (_rms(o, sub_ref[...]) * (1.0 - lambda_init)).astype(o_ref.dtype)

    _attend(q_ref, k_ref, vt_ref, o_ref, scratch, finish, n_maps=2, **kw)


def _attention(kernel, q, k, vt, *, n_batch, n_heads, n_maps, dqk, dv, ctx_len, tq, chunk, extra=(), name):
    n = q.shape[0]
    tokens = n // n_batch
    n_chunks = tokens // chunk
    n_lanes = ATTN_STREAMS * n_maps
    extra_specs = [pl.BlockSpec(e.shape, lambda b, h: (0, 0)) for e in extra]
    return pl.pallas_call(
        functools.partial(kernel, n_chunks=n_chunks, chunk=chunk, ctx_len=ctx_len, tq=tq),
        grid=(n_batch, n_heads),
        in_specs=extra_specs + [
            pl.BlockSpec((tokens, dqk), lambda b, h: (b, h)),
            pl.BlockSpec((tokens, dqk), lambda b, h: (b, h)),
            pl.BlockSpec((n_chunks, dv, chunk), lambda b, h: (b, h, 0)),
        ],
        out_specs=pl.BlockSpec((tokens, dv), lambda b, h: (b, h)),
        out_shape=jax.ShapeDtypeStruct((n, n_heads * dv), BF16),
        scratch_shapes=[pltpu.VMEM((n_lanes, chunk, tq), F32), pltpu.VMEM((n_lanes, chunk, tq), F32),
                        pltpu.VMEM((n_lanes, dv, tq), F32)],
        compiler_params=_params("arbitrary", "arbitrary"),
        name=name,
    )(*extra, q, k, vt)


def _rope_tables(ctx_len, seq, rot_dim):
    half = rot_dim // 2
    n_freq = half // 2
    t = jnp.arange(seq, dtype=jnp.int32)
    pos_row = (t // GRID_W).astype(F32)
    pos_col = (t % GRID_W).astype(F32)
    inv = ROPE_BASE ** (-jnp.arange(0, half, 2, dtype=F32) / half)
    lane = jnp.arange(LANES)
    freq = inv[(lane % half) % n_freq]
    pos = jnp.where((lane // half)[None, :] == 0, pos_row[:, None], pos_col[:, None])
    ang = pos * freq[None, :]
    live = (lane < rot_dim)[None, :]
    sign = jnp.where((lane % half) < n_freq, -1.0, 1.0)[None, :]
    cos = jnp.where(live, jnp.cos(ang), 1.0)
    sin = jnp.where(live, jnp.sin(ang) * sign, 0.0)
    cos = jnp.concatenate([jnp.ones((ctx_len, LANES), F32), cos], axis=0)
    sin = jnp.concatenate([jnp.zeros((ctx_len, LANES), F32), sin], axis=0)
    return cos, sin


def kernel(x, c, ctx, c_ctx, ada_w, ada_b, norm_g, mla_w_dq, mla_q_norm, mla_w_uq, mla_w_dkv, mla_kv_norm, mla_w_ukv, mla_w_o, diff_w_qkv, diff_lambda, diff_subln, diff_w_o, ffn_w_gu, ffn_w_down, moe_router, moe_w_gu, moe_w_down):
    n_batch, seq, d = x.shape
    ctx_len = ctx.shape[1]
    depth = ada_w.shape[0]
    tokens = ctx_len + seq
    tm = TOKEN_TILE
    assert tokens % tm == 0 and tokens % ctx_len == 0 and ctx_len % LANES == 0 and ctx_len <= tm
    assert n_batch < MOD_ROWS
    diff_heads = d // (2 * DIFF_HEAD_DIM)

    xs = jnp.concatenate([ctx, x], axis=1).reshape(n_batch * tokens, d)
    cond = jnp.zeros((MOD_ROWS, d), F32).at[:n_batch].set(c).at[n_batch].set(c_ctx)
    ada_b3 = ada_b.reshape(depth, 1, N_MOD * d)
    stream = functools.partial(_stream, n_batch=n_batch, tile=ctx_len)

    cos_d, sin_d = _rope_tables(ctx_len, seq, DIFF_HEAD_DIM)
    cos_m, sin_m = _rope_tables(ctx_len, seq, MLA_ROPE_DIM)

    mods = _ada(cond, ada_w, ada_b3, 0)
    (h,) = stream(xs, g_norm=norm_g[0, 0:1], mods_norm=mods, shift_idx=0)

    for i in range(depth):
        j = i // 2
        g = norm_g[i]
        if i % 2 == 0:
            w_cat = jnp.concatenate(
                [mla_w_dq[j], mla_w_dkv[j], jnp.zeros((d, LANES - MLA_ROPE_DIM), F32)], axis=1).astype(BF16)
            w_uq = mla_w_uq[j].reshape(MLA_Q_RANK, MLA_HEADS, MLA_NOPE_DIM + MLA_ROPE_DIM)
            w_uq = jnp.pad(w_uq, ((0, 0), (0, 0), (0, MLA_QK_PAD - w_uq.shape[2])))
            w_uq = w_uq.reshape(MLA_Q_RANK, MLA_HEADS * MLA_QK_PAD).astype(BF16)
            w_ukv = mla_w_ukv[j].reshape(MLA_KV_RANK, MLA_HEADS, MLA_NOPE_DIM + MLA_V_DIM)
            w_uk = w_ukv[:, :, :MLA_NOPE_DIM].reshape(MLA_KV_RANK, -1).astype(BF16)
            w_uv_t = w_ukv[:, :, MLA_NOPE_DIM:].reshape(MLA_KV_RANK, -1).T.astype(BF16)

            cq, cn, kpe = _mla_down(h, w_cat, mla_q_norm[j][None], mla_kv_norm[j][None], cos_m, sin_m, tm=tm)
            q = _rope_matmul(cq, w_uq, cos_m, sin_m, tm=tm, tn=2048, half=MLA_ROPE_DIM // 4,
                             rope_every=2, scale=MLA_SCALE * LOG2E, name="mla_q")
            k = _mla_kup(cn, w_uk, kpe, tm=tm)
            vt = _vt_matmul(w_uv_t, cn, chunk=tm, tv=w_uv_t.shape[0])
            o = _attention(_mla_attn_kernel, q, k, vt, n_batch=n_batch, n_heads=MLA_HEADS, n_maps=1,
                           dqk=MLA_QK_PAD, dv=MLA_V_DIM, ctx_len=ctx_len, tq=ATTN_Q_TILE, chunk=tm,
                           name="mla_attn")
            y = _matmul(o, mla_w_o[j].astype(BF16), tm=tm, tn=1024, name="mla_o")
        else:
            lambda_init = 0.8 - 0.6 * math.exp(-0.3 * i)
            w_q = diff_w_qkv[j][:, :d].astype(BF16)
            w_k = diff_w_qkv[j][:, d:2 * d].astype(BF16)
            w_v_t = diff_w_qkv[j][:, 2 * d:].T.astype(BF16)
            scale = DIFF_HEAD_DIM ** -0.5 * LOG2E
            q = _rope_matmul(h, w_q, cos_d, sin_d, tm=tm, tn=1024, half=DIFF_HEAD_DIM // 4,
                             rope_every=1, scale=scale, name="diff_q")
            k = _rope_matmul(h, w_k, cos_d, sin_d, tm=tm, tn=1024, half=DIFF_HEAD_DIM // 4,
                             rope_every=1, scale=1.0, name="diff_k")
            vt = _vt_matmul(w_v_t, h, chunk=tm, tv=1024)
            o = _attention(functools.partial(_diff_attn_kernel, lambda_init=lambda_init), q, k, vt,
                           n_batch=n_batch, n_heads=diff_heads, n_maps=2, dqk=2 * DIFF_HEAD_DIM,
                           dv=2 * DIFF_HEAD_DIM, ctx_len=ctx_len, tq=ATTN_Q_TILE, chunk=tm,
                           extra=(diff_lambda[j], diff_subln[j][None]), name="diff_attn")
            y = _matmul(o, diff_w_o[j].astype(BF16), tm=tm, tn=1024, name="diff_o")

        if i % 2 == 0:
            xs, h = stream(xs, y=y, g_res=g[1:2], mods_res=mods, gate_idx=2,
                           g_norm=g[2:3], mods_norm=mods, shift_idx=3)
            a = _swiglu_dense(h, ffn_w_gu[j].astype(BF16), tm=tm, tn=512)
            f = _matmul(a, ffn_w_down[j].astype(BF16), tm=tm, tn=1024, name="ffn_down")
        else:
            w_r = jnp.pad(moe_router[j], ((0, 0), (0, LANES - N_EXPERTS)))
            xs, h, comb = stream(xs, y=y, g_res=g[1:2], mods_res=mods, gate_idx=2,
                                 g_norm=g[2:3], mods_norm=mods, shift_idx=3, w_router=w_r)
            a = _swiglu_experts(h, moe_w_gu[j].astype(BF16), comb, tm=tm, tn=512)
            w_down = moe_w_down[j].reshape(-1, d).astype(BF16)
            f = _matmul(a, w_down, tm=tm, tn=512, name="moe_down")

        if i + 1 < depth:
            mods_next = _ada(cond, ada_w, ada_b3, i + 1)
            xs, h = stream(xs, y=f, g_res=g[3:4], mods_res=mods, gate_idx=5,
                           g_norm=norm_g[i + 1, 0:1], mods_norm=mods_next, shift_idx=0)
            mods = mods_next
        else:
            (xs,) = stream(xs, y=f, g_res=g[3:4], mods_res=mods, gate_idx=5)

    return xs.reshape(n_batch, tokens, d)[:, ctx_len:]
```

```python
import functools
import math

import jax
import jax.numpy as jnp
from jax import lax
from jax.experimental import pallas as pl
from jax.experimental.pallas import tpu as pltpu

F32 = jnp.float32
BF16 = jnp.bfloat16

GRID_W = 64
ROPE_BASE = 10000.0
EPS = 1e-6
N_MOD = 6

MLA_HEADS = 32
MLA_Q_RANK = 1024
MLA_KV_RANK = 512
MLA_NOPE_DIM = 128
MLA_ROPE_DIM = 64
MLA_V_DIM = 128
MLA_QK_PAD = 256
MLA_SCALE = (MLA_NOPE_DIM + MLA_ROPE_DIM) ** -0.5

DIFF_HEAD_DIM = 128

N_EXPERTS = 8

LANES = 128
MOD_ROWS = 8
TOKEN_TILE = 768
ATTN_Q_TILE = 512
ATTN_STREAMS = 2
ONES_ROWS = 16
OVERSHOOT_LIMIT = 60.0
VMEM_LIMIT = 56 * 1024 * 1024
LOG2E = math.log2(math.e)


def _params(*semantics):
    return pltpu.CompilerParams(dimension_semantics=semantics, vmem_limit_bytes=VMEM_LIMIT)


def _rms(x, g):
    return x * lax.rsqrt(jnp.mean(x * x, axis=-1, keepdims=True) + EPS) * g


def _swap_halves(x, half):
    lane = lax.broadcasted_iota(jnp.int32, x.shape, 1)
    first = (lane % (2 * half)) < half
    return jnp.where(first, pltpu.roll(x, LANES - half, 1), pltpu.roll(x, half, 1))


def _rope(x, cos, sin, half):
    return x * cos + _swap_halves(x, half) * sin


def _ada_kernel(c_ref, w_ref, b_ref, o_ref):
    c = c_ref[...]
    s = (c * jax.nn.sigmoid(c)).astype(BF16)
    o_ref[...] = jnp.dot(s, w_ref[...].astype(BF16), preferred_element_type=F32) + b_ref[...]


def _ada(cond, ada_w, ada_b3, layer):
    d = cond.shape[1]
    n = ada_w.shape[2]
    tn = 512
    return pl.pallas_call(
        _ada_kernel,
        grid=(n // tn,),
        in_specs=[
            pl.BlockSpec((MOD_ROWS, d), lambda j: (0, 0)),
            pl.BlockSpec((None, d, tn), lambda j: (layer, 0, j)),
            pl.BlockSpec((None, 1, tn), lambda j: (layer, 0, j)),
        ],
        out_specs=pl.BlockSpec((MOD_ROWS, tn), lambda j: (0, j)),
        out_shape=jax.ShapeDtypeStruct((MOD_ROWS, n), F32),
        compiler_params=_params("arbitrary"),
        name="ada",
    )(cond, ada_w, ada_b3)


def _stream_kernel(*refs, n_batch, d, resid, norm, router, gate_idx, shift_idx):
    refs = list(refs)
    x_ref = refs.pop(0)
    if resid:
        y_ref, gres_ref, mres_ref = refs.pop(0), refs.pop(0), refs.pop(0)
    if norm:
        gnorm_ref, mnorm_ref = refs.pop(0), refs.pop(0)
    if router:
        wr_ref = refs.pop(0)
    if resid:
        xo_ref = refs.pop(0)
    if norm:
        h_ref = refs.pop(0)
    if router:
        comb_ref = refs.pop(0)

    row = jnp.where(pl.program_id(1) == 0, n_batch, pl.program_id(0))

    def mod(ref, k):
        return ref[pl.ds(row, 1), pl.ds(k * d, d)]

    x = x_ref[...]
    if resid:
        x = x + mod(mres_ref, gate_idx) * _rms(y_ref[...].astype(F32), gres_ref[...])
        xo_ref[...] = x
    if norm:
        h = _rms(x, gnorm_ref[...]) * (1.0 + mod(mnorm_ref, shift_idx + 1)) + mod(mnorm_ref, shift_idx)
        h_ref[...] = h.astype(BF16)
    if router:
        logits = jnp.dot(h, wr_ref[...], precision=lax.Precision.HIGHEST, preferred_element_type=F32)
        lane = lax.broadcasted_iota(jnp.int32, logits.shape, 1)
        neg = jnp.float32(-jnp.inf)
        lg = jnp.where(lane < N_EXPERTS, logits, neg)
        m1 = jnp.max(lg, axis=1, keepdims=True)
        i1 = jnp.min(jnp.where(lg == m1, lane, LANES), axis=1, keepdims=True)
        lg2 = jnp.where(lane == i1, neg, lg)
        m2 = jnp.max(lg2, axis=1, keepdims=True)
        i2 = jnp.min(jnp.where(lg2 == m2, lane, LANES), axis=1, keepdims=True)
        e2 = jnp.exp(m2 - m1)
        w1 = 1.0 / (1.0 + e2)
        w2 = e2 / (1.0 + e2)
        comb_ref[...] = jnp.where(lane == i1, w1, jnp.where(lane == i2, w2, 0.0))


def _stream(x, *, n_batch, tile, y=None, g_res=None, mods_res=None, gate_idx=0,
            g_norm=None, mods_norm=None, shift_idx=0, w_router=None):
    n, d = x.shape
    resid, norm, router = y is not None, g_norm is not None, w_router is not None
    tpb = n // n_batch // tile
    row_spec = pl.BlockSpec((tile, d), lambda b, t: (b * tpb + t, 0))
    vec_spec = pl.BlockSpec((1, d), lambda b, t: (0, 0))
    mod_spec = pl.BlockSpec((MOD_ROWS, N_MOD * d), lambda b, t: (0, 0))
    args, in_specs, out_shape, out_specs = [x], [row_spec], [], []
    if resid:
        args += [y, g_res, mods_res]
        in_specs += [row_spec, vec_spec, mod_spec]
        out_shape.append(jax.ShapeDtypeStruct((n, d), F32))
        out_specs.append(row_spec)
    if norm:
        args += [g_norm, mods_norm]
        in_specs += [vec_spec, mod_spec]
        out_shape.append(jax.ShapeDtypeStruct((n, d), BF16))
        out_specs.append(row_spec)
    if router:
        args.append(w_router)
        in_specs.append(pl.BlockSpec((d, LANES), lambda b, t: (0, 0)))
        out_shape.append(jax.ShapeDtypeStruct((n, LANES), F32))
        out_specs.append(pl.BlockSpec((tile, LANES), lambda b, t: (b * tpb + t, 0)))
    kernel = functools.partial(_stream_kernel, n_batch=n_batch, d=d, resid=resid, norm=norm,
                               router=router, gate_idx=gate_idx, shift_idx=shift_idx)
    return pl.pallas_call(
        kernel,
        grid=(n_batch, tpb),
        in_specs=in_specs,
        out_specs=out_specs,
        out_shape=out_shape,
        input_output_aliases={0: 0} if resid else {},
        compiler_params=_params("arbitrary", "arbitrary"),
        name="stream",
    )(*args)


def _mm_kernel(x_ref, w_ref, o_ref):
    o_ref[...] = jnp.dot(x_ref[...], w_ref[...], preferred_element_type=F32).astype(o_ref.dtype)


def _matmul(x, w, *, tm, tn, out_dtype=BF16, name="mm"):
    m, k = x.shape
    n = w.shape[1]
    tn = min(tn, n)
    return pl.pallas_call(
        _mm_kernel,
        grid=(m // tm, n // tn),
        in_specs=[pl.BlockSpec((tm, k), lambda i, j: (i, 0)),
                  pl.BlockSpec((k, tn), lambda i, j: (0, j))],
        out_specs=pl.BlockSpec((tm, tn), lambda i, j: (i, j)),
        out_shape=jax.ShapeDtypeStruct((m, n), out_dtype),
        compiler_params=_params("arbitrary", "arbitrary"),
        name=name,
    )(x, w)


def _swiglu_kernel(*refs, tiles_per_expert):
    if tiles_per_expert:
        x_ref, wg_ref, wu_ref, comb_ref, o_ref = refs
    else:
        x_ref, wg_ref, wu_ref, o_ref = refs
    x = x_ref[...]
    g = jnp.dot(x, wg_ref[...], preferred_element_type=F32)
    u = jnp.dot(x, wu_ref[...], preferred_element_type=F32)
    a = g * jax.nn.sigmoid(g) * u
    if tiles_per_expert:
        e = pl.program_id(1) // tiles_per_expert
        comb = comb_ref[...]
        lane = lax.broadcasted_iota(jnp.int32, comb.shape, 1)
        a = a * jnp.sum(jnp.where(lane == e, comb, 0.0), axis=1, keepdims=True)
    o_ref[...] = a.astype(o_ref.dtype)


def _swiglu_dense(x, w_gu, *, tm, tn):
    m, k = x.shape
    f = w_gu.shape[1] // 2
    nj = f // tn
    return pl.pallas_call(
        functools.partial(_swiglu_kernel, tiles_per_expert=0),
        grid=(m // tm, nj),
        in_specs=[pl.BlockSpec((tm, k), lambda i, j: (i, 0)),
                  pl.BlockSpec((k, tn), lambda i, j: (0, j)),
                  pl.BlockSpec((k, tn), lambda i, j: (0, nj + j))],
        out_specs=pl.BlockSpec((tm, tn), lambda i, j: (i, j)),
        out_shape=jax.ShapeDtypeStruct((m, f), BF16),
        compiler_params=_params("arbitrary", "arbitrary"),
        name="swiglu",
    )(x, w_gu, w_gu)


def _swiglu_experts(x, w_gu, comb, *, tm, tn):
    m, k = x.shape
    n_exp, _, f2 = w_gu.shape
    f = f2 // 2
    tpe = f // tn
    return pl.pallas_call(
        functools.partial(_swiglu_kernel, tiles_per_expert=tpe),
        grid=(m // tm, n_exp * tpe),
        in_specs=[pl.BlockSpec((tm, k), lambda i, j: (i, 0)),
                  pl.BlockSpec((None, k, tn), lambda i, j: (j // tpe, 0, j % tpe)),
                  pl.BlockSpec((None, k, tn), lambda i, j: (j // tpe, 0, tpe + j % tpe)),
                  pl.BlockSpec((tm, LANES), lambda i, j: (i, 0))],
        out_specs=pl.BlockSpec((tm, tn), lambda i, j: (i, j)),
        out_shape=jax.ShapeDtypeStruct((m, n_exp * f), BF16),
        compiler_params=_params("arbitrary", "arbitrary"),
        name="swiglu_experts",
    )(x, w_gu, w_gu, comb)


def _rope_mm_kernel(x_ref, w_ref, cos_ref, sin_ref, o_ref, *, half, rope_every, scale):
    acc = jnp.dot(x_ref[...], w_ref[...], preferred_element_type=F32)
    cos, sin = cos_ref[...], sin_ref[...]
    for g in range(acc.shape[1] // LANES):
        blk = acc[:, g * LANES:(g + 1) * LANES]
        if g % rope_every == rope_every - 1:
            blk = _rope(blk, cos, sin, half)
        o_ref[:, g * LANES:(g + 1) * LANES] = (blk * scale).astype(o_ref.dtype)


def _rope_matmul(x, w, cos, sin, *, tm, tn, half, rope_every, scale, name):
    m, k = x.shape
    n = w.shape[1]
    tn = min(tn, n)
    tpb = cos.shape[0] // tm
    return pl.pallas_call(
        functools.partial(_rope_mm_kernel, half=half, rope_every=rope_every, scale=scale),
        grid=(m // tm, n // tn),
        in_specs=[pl.BlockSpec((tm, k), lambda i, j: (i, 0)),
                  pl.BlockSpec((k, tn), lambda i, j: (0, j)),
                  pl.BlockSpec((tm, LANES), lambda i, j: (i % tpb, 0)),
                  pl.BlockSpec((tm, LANES), lambda i, j: (i % tpb, 0))],
        out_specs=pl.BlockSpec((tm, tn), lambda i, j: (i, j)),
        out_shape=jax.ShapeDtypeStruct((m, n), BF16),
        compiler_params=_params("arbitrary", "arbitrary"),
        name=name,
    )(x, w, cos, sin)


def _vt_kernel(w_ref, x_ref, o_ref, *, dv):
    res = lax.dot_general(w_ref[...], x_ref[...], (((1,), (1,)), ((), ())),
                          preferred_element_type=F32).astype(o_ref.dtype)
    dvp = dv + ONES_ROWS
    ones = jnp.ones((ONES_ROWS, res.shape[1]), o_ref.dtype)
    for h in range(res.shape[0] // dv):
        o_ref[h * dvp:h * dvp + dv, :] = res[h * dv:(h + 1) * dv]
        o_ref[h * dvp + dv:(h + 1) * dvp, :] = ones


def _vt_matmul(w_t, x, *, chunk, tv, dv):
    d_all, k = w_t.shape
    n = x.shape[0]
    tv = min(tv, d_all)
    tvp = tv // dv * (dv + ONES_ROWS)
    return pl.pallas_call(
        functools.partial(_vt_kernel, dv=dv),
        grid=(n // chunk, d_all // tv),
        in_specs=[pl.BlockSpec((tv, k), lambda c, j: (j, 0)),
                  pl.BlockSpec((chunk, k), lambda c, j: (c, 0))],
        out_specs=pl.BlockSpec((None, tvp, chunk), lambda c, j: (c, j, 0)),
        out_shape=jax.ShapeDtypeStruct((n // chunk, d_all // dv * (dv + ONES_ROWS), chunk), BF16),
        compiler_params=_params("arbitrary", "arbitrary"),
        name="vt",
    )(w_t, x)


def _mla_down_kernel(x_ref, w_ref, qn_ref, kvn_ref, cos_ref, sin_ref, cq_ref, cn_ref, kpe_ref):
    acc = jnp.dot(x_ref[...], w_ref[...], preferred_element_type=F32)
    r0, r1 = MLA_Q_RANK, MLA_Q_RANK + MLA_KV_RANK
    cq_ref[...] = _rms(acc[:, :r0], qn_ref[...]).astype(BF16)
    cn_ref[...] = _rms(acc[:, r0:r1], kvn_ref[...]).astype(BF16)
    kpe_ref[...] = _rope(acc[:, r1:], cos_ref[...], sin_ref[...], MLA_ROPE_DIM // 4).astype(BF16)


def _mla_down(h, w_cat, q_norm, kv_norm, cos, sin, *, tm):
    m, k = h.shape
    n = w_cat.shape[1]
    tpb = cos.shape[0] // tm
    row = lambda width: pl.BlockSpec((tm, width), lambda i: (i, 0))
    tab = pl.BlockSpec((tm, LANES), lambda i: (i % tpb, 0))
    return pl.pallas_call(
        _mla_down_kernel,
        grid=(m // tm,),
        in_specs=[row(k),
                  pl.BlockSpec((k, n), lambda i: (0, 0)),
                  pl.BlockSpec((1, MLA_Q_RANK), lambda i: (0, 0)),
                  pl.BlockSpec((1, MLA_KV_RANK), lambda i: (0, 0)),
                  tab, tab],
        out_specs=[row(MLA_Q_RANK), row(MLA_KV_RANK), row(LANES)],
        out_shape=[jax.ShapeDtypeStruct((m, MLA_Q_RANK), BF16),
                   jax.ShapeDtypeStruct((m, MLA_KV_RANK), BF16),
                   jax.ShapeDtypeStruct((m, LANES), BF16)],
        compiler_params=_params("arbitrary"),
        name="mla_down",
    )(h, w_cat, q_norm, kv_norm, cos, sin)


def _mla_kup_kernel(c_ref, w_ref, kpe_ref, o_ref):
    acc = jnp.dot(c_ref[...], w_ref[...], preferred_element_type=F32).astype(BF16)
    kpe = kpe_ref[...]
    for h in range(MLA_HEADS):
        o_ref[:, h * MLA_QK_PAD:h * MLA_QK_PAD + LANES] = acc[:, h * LANES:(h + 1) * LANES]
        o_ref[:, h * MLA_QK_PAD + LANES:(h + 1) * MLA_QK_PAD] = kpe


def _mla_kup(cn, w_uk, kpe, *, tm):
    m, k = cn.shape
    n_out = MLA_HEADS * MLA_QK_PAD
    return pl.pallas_call(
        _mla_kup_kernel,
        grid=(m // tm,),
        in_specs=[pl.BlockSpec((tm, k), lambda i: (i, 0)),
                  pl.BlockSpec((k, MLA_HEADS * MLA_NOPE_DIM), lambda i: (0, 0)),
                  pl.BlockSpec((tm, LANES), lambda i: (i, 0))],
        out_specs=pl.BlockSpec((tm, n_out), lambda i: (i, 0)),
        out_shape=jax.ShapeDtypeStruct((m, n_out), BF16),
        compiler_params=_params("arbitrary"),
        name="mla_kup",
    )(cn, w_uk, kpe)


def _fold_rows(op, s):
    r = s.shape[0]
    while r > 8 and r % 16 == 0:
        r //= 2
        s = op(s[:r], s[r:])
    return s


def _scores(k, q):
    return lax.dot_general(k, q, (((1,), (1,)), ((), ())), preferred_element_type=F32)


def _attend(q_ref, k_ref, vt_ref, o_ref, acc_ref, finish, *, n_maps, n_chunks, chunk, ctx_len, tq):
    tokens, dqk = q_ref.shape
    dv = vt_ref.shape[1] - ONES_ROWS
    dm = dqk // n_maps
    n_lanes = ATTN_STREAMS * n_maps

    def colmax(s):
        return jnp.max(_fold_rows(jnp.maximum, s), axis=0, keepdims=True)

    def split(acc):
        return acc[dv:dv + 1], acc[:dv]

    def pv(v_t, p):
        return jnp.dot(v_t, p.astype(BF16), preferred_element_type=F32)

    parts = []
    for a in range(n_maps):
        s = _scores(k_ref[0:ctx_len, a * dm:(a + 1) * dm], q_ref[0:ctx_len, a * dm:(a + 1) * dm])
        parts.append(split(pv(vt_ref[0, :, 0:ctx_len], jnp.exp2(s - colmax(s)))))
    o_ref[0:ctx_len, :] = finish(parts)

    def q_body(i, _):
        offs = [pl.multiple_of(ctx_len + (i * ATTN_STREAMS + j) * tq, math.gcd(ctx_len, tq))
                for j in range(ATTN_STREAMS)]

        def lane_scores(c, w):
            j, a = divmod(w, n_maps)
            k = k_ref[pl.ds(pl.multiple_of(c * chunk, chunk), chunk), a * dm:(a + 1) * dm]
            return _scores(k, q_ref[pl.ds(offs[j], tq), a * dm:(a + 1) * dm])

        def write_out():
            for j, off in enumerate(offs):
                o_ref[pl.ds(off, tq), :] = finish([split(acc_ref[j * n_maps + a]) for a in range(n_maps)])

        shifts = []
        for w in range(n_lanes):
            s = lane_scores(0, w)
            m = colmax(s)
            acc_ref[w] = pv(vt_ref[0], jnp.exp2(s - m))
            shifts.append(m)

        def fast_body(c, carry):
            shifts, over = carry
            v_t = vt_ref[c]
            out_shifts, out_over = [], []
            for w in range(n_lanes):
                s = lane_scores(c, w)
                m_old = shifts[w]
                cm = colmax(s)
                m_new = jnp.maximum(m_old, cm)
                acc_ref[w] = (acc_ref[w] + pv(v_t, jnp.exp2(s - m_old))) * jnp.exp2(m_old - m_new)
                out_shifts.append(m_new)
                out_over.append(jnp.maximum(over[w], cm - m_old))
            return tuple(out_shifts), tuple(out_over)

        zero = jnp.zeros((1, tq), F32)
        _, over = lax.fori_loop(1, n_chunks, fast_body, (tuple(shifts), (zero,) * n_lanes))
        write_out()

        worst = over[0]
        for w in range(1, n_lanes):
            worst = jnp.maximum(worst, over[w])

        @pl.when(jnp.max(worst) > OVERSHOOT_LIMIT)
        def _():
            def safe_body(c, ms):
                v_t = vt_ref[c]
                out = []
                for w in range(n_lanes):
                    s = lane_scores(c, w)
                    m_new = jnp.maximum(ms[w], colmax(s))
                    acc_ref[w] = acc_ref[w] * jnp.exp2(ms[w] - m_new) + pv(v_t, jnp.exp2(s - m_new))
                    out.append(m_new)
                return tuple(out)

            acc_ref[...] = jnp.zeros(acc_ref.shape, F32)
            lax.fori_loop(0, n_chunks, safe_body, (jnp.full((1, tq), -jnp.inf, F32),) * n_lanes)
            write_out()

        return 0

    lax.fori_loop(0, (tokens - ctx_len) // (tq * ATTN_STREAMS), q_body, 0)


def _mla_attn_kernel(q_ref, k_ref, vt_ref, o_ref, acc_ref, **kw):
    def finish(parts):
        ((l, acc),) = parts
        return (acc / l).T.astype(o_ref.dtype)

    _attend(q_ref, k_ref, vt_ref, o_ref, acc_ref, finish, n_maps=1, **kw)


def _diff_attn_kernel(lam_ref, sub_ref, q_ref, k_ref, vt_ref, o_ref, acc_ref, *, lambda_init, **kw):
    def finish(parts):
        (l1, a1), (l2, a2) = parts
        lp = lam_ref[...]
        lam = (jnp.exp(jnp.sum(lp[0:1] * lp[1:2], keepdims=True))
               - jnp.exp(jnp.sum(lp[2:3] * lp[3:4], keepdims=True)) + lambda_init)
        o = (a1 / l1 - lam * (a2 / l2)).T
        return (_rms(o, sub_ref[...]) * (1.0 - lambda_init)).astype(o_ref.dtype)

    _attend(q_ref, k_ref, vt_ref, o_ref, acc_ref, finish, n_maps=2, **kw)


def _attention(kernel, q, k, vt, *, n_batch, n_heads, n_maps, dqk, dv, ctx_len, tq, chunk, extra=(), name):
    n = q.shape[0]
    tokens = n // n_batch
    n_chunks = tokens // chunk
    n_lanes = ATTN_STREAMS * n_maps
    dvp = dv + ONES_ROWS
    extra_specs = [pl.BlockSpec(e.shape, lambda b, h: (0, 0)) for e in extra]
    return pl.pallas_call(
        functools.partial(kernel, n_chunks=n_chunks, chunk=chunk, ctx_len=ctx_len, tq=tq),
        grid=(n_batch, n_heads),
        in_specs=extra_specs + [
            pl.BlockSpec((tokens, dqk), lambda b, h: (b, h)),
            pl.BlockSpec((tokens, dqk), lambda b, h: (b, h)),
            pl.BlockSpec((n_chunks, dvp, chunk), lambda b, h: (b, h, 0)),
        ],
        out_specs=pl.BlockSpec((tokens, dv), lambda b, h: (b, h)),
        out_shape=jax.ShapeDtypeStruct((n, n_heads * dv), BF16),
        scratch_shapes=[pltpu.VMEM((n_lanes, dvp, tq), F32)],
        compiler_params=_params("arbitrary", "arbitrary"),
        name=name,
    )(*extra, q, k, vt)


def _rope_tables(ctx_len, seq, rot_dim):
    half = rot_dim // 2
    n_freq = half // 2
    t = jnp.arange(seq, dtype=jnp.int32)
    pos_row = (t // GRID_W).astype(F32)
    pos_col = (t % GRID_W).astype(F32)
    inv = ROPE_BASE ** (-jnp.arange(0, half, 2, dtype=F32) / half)
    lane = jnp.arange(LANES)
    freq = inv[(lane % half) % n_freq]
    pos = jnp.where((lane // half)[None, :] == 0, pos_row[:, None], pos_col[:, None])
    ang = pos * freq[None, :]
    live = (lane < rot_dim)[None, :]
    sign = jnp.where((lane % half) < n_freq, -1.0, 1.0)[None, :]
    cos = jnp.where(live, jnp.cos(ang), 1.0)
    sin = jnp.where(live, jnp.sin(ang) * sign, 0.0)
    cos = jnp.concatenate([jnp.ones((ctx_len, LANES), F32), cos], axis=0)
    sin = jnp.concatenate([jnp.zeros((ctx_len, LANES), F32), sin], axis=0)
    return cos, sin


def kernel(x, c, ctx, c_ctx, ada_w, ada_b, norm_g, mla_w_dq, mla_q_norm, mla_w_uq, mla_w_dkv, mla_kv_norm, mla_w_ukv, mla_w_o, diff_w_qkv, diff_lambda, diff_subln, diff_w_o, ffn_w_gu, ffn_w_down, moe_router, moe_w_gu, moe_w_down):
    n_batch, seq, d = x.shape
    ctx_len = ctx.shape[1]
    depth = ada_w.shape[0]
    tokens = ctx_len + seq
    tm = TOKEN_TILE
    assert tokens % tm == 0 and tokens % ctx_len == 0 and ctx_len % LANES == 0 and ctx_len <= tm
    assert n_batch < MOD_ROWS
    diff_heads = d // (2 * DIFF_HEAD_DIM)

    xs = jnp.concatenate([ctx, x], axis=1).reshape(n_batch * tokens, d)
    cond = jnp.zeros((MOD_ROWS, d), F32).at[:n_batch].set(c).at[n_batch].set(c_ctx)
    ada_b3 = ada_b.reshape(depth, 1, N_MOD * d)
    stream = functools.partial(_stream, n_batch=n_batch, tile=ctx_len)

    cos_d, sin_d = _rope_tables(ctx_len, seq, DIFF_HEAD_DIM)
    cos_m, sin_m = _rope_tables(ctx_len, seq, MLA_ROPE_DIM)

    mods = _ada(cond, ada_w, ada_b3, 0)
    (h,) = stream(xs, g_norm=norm_g[0, 0:1], mods_norm=mods, shift_idx=0)

    for i in range(depth):
        j = i // 2
        g = norm_g[i]
        if i % 2 == 0:
            w_cat = jnp.concatenate(
                [mla_w_dq[j], mla_w_dkv[j], jnp.zeros((d, LANES - MLA_ROPE_DIM), F32)], axis=1).astype(BF16)
            w_uq = mla_w_uq[j].reshape(MLA_Q_RANK, MLA_HEADS, MLA_NOPE_DIM + MLA_ROPE_DIM)
            w_uq = jnp.pad(w_uq, ((0, 0), (0, 0), (0, MLA_QK_PAD - w_uq.shape[2])))
            w_uq = w_uq.reshape(MLA_Q_RANK, MLA_HEADS * MLA_QK_PAD).astype(BF16)
            w_ukv = mla_w_ukv[j].reshape(MLA_KV_RANK, MLA_HEADS, MLA_NOPE_DIM + MLA_V_DIM)
            w_uk = w_ukv[:, :, :MLA_NOPE_DIM].reshape(MLA_KV_RANK, -1).astype(BF16)
            w_uv_t = w_ukv[:, :, MLA_NOPE_DIM:].reshape(MLA_KV_RANK, -1).T.astype(BF16)

            cq, cn, kpe = _mla_down(h, w_cat, mla_q_norm[j][None], mla_kv_norm[j][None], cos_m, sin_m, tm=tm)
            q = _rope_matmul(cq, w_uq, cos_m, sin_m, tm=tm, tn=2048, half=MLA_ROPE_DIM // 4,
                             rope_every=2, scale=MLA_SCALE * LOG2E, name="mla_q")
            k = _mla_kup(cn, w_uk, kpe, tm=tm)
            vt = _vt_matmul(w_uv_t, cn, chunk=tm, tv=w_uv_t.shape[0], dv=MLA_V_DIM)
            o = _attention(_mla_attn_kernel, q, k, vt, n_batch=n_batch, n_heads=MLA_HEADS, n_maps=1,
                           dqk=MLA_QK_PAD, dv=MLA_V_DIM, ctx_len=ctx_len, tq=ATTN_Q_TILE, chunk=tm,
                           name="mla_attn")
            y = _matmul(o, mla_w_o[j].astype(BF16), tm=tm, tn=1024, name="mla_o")
        else:
            lambda_init = 0.8 - 0.6 * math.exp(-0.3 * i)
            w_q = diff_w_qkv[j][:, :d].astype(BF16)
            w_k = diff_w_qkv[j][:, d:2 * d].astype(BF16)
            w_v_t = diff_w_qkv[j][:, 2 * d:].T.astype(BF16)
            scale = DIFF_HEAD_DIM ** -0.5 * LOG2E
            q = _rope_matmul(h, w_q, cos_d, sin_d, tm=tm, tn=1024, half=DIFF_HEAD_DIM // 4,
                             rope_every=1, scale=scale, name="diff_q")
            k = _rope_matmul(h, w_k, cos_d, sin_d, tm=tm, tn=1024, half=DIFF_HEAD_DIM // 4,
                             rope_every=1, scale=1.0, name="diff_k")
            vt = _vt_matmul(w_v_t, h, chunk=tm, tv=1024, dv=2 * DIFF_HEAD_DIM)
            o = _attention(functools.partial(_diff_attn_kernel, lambda_init=lambda_init), q, k, vt,
                           n_batch=n_batch, n_heads=diff_heads, n_maps=2, dqk=2 * DIFF_HEAD_DIM,
                           dv=2 * DIFF_HEAD_DIM, ctx_len=ctx_len, tq=ATTN_Q_TILE, chunk=tm,
                           extra=(diff_lambda[j], diff_subln[j][None]), name="diff_attn")
            y = _matmul(o, diff_w_o[j].astype(BF16), tm=tm, tn=1024, name="diff_o")

        if i % 2 == 0:
            xs, h = stream(xs, y=y, g_res=g[1:2], mods_res=mods, gate_idx=2,
                           g_norm=g[2:3], mods_norm=mods, shift_idx=3)
            a = _swiglu_dense(h, ffn_w_gu[j].astype(BF16), tm=tm, tn=512)
            f = _matmul(a, ffn_w_down[j].astype(BF16), tm=tm, tn=1024, name="ffn_down")
        else:
            w_r = jnp.pad(moe_router[j], ((0, 0), (0, LANES - N_EXPERTS)))
            xs, h, comb = stream(xs, y=y, g_res=g[1:2], mods_res=mods, gate_idx=2,
                                 g_norm=g[2:3], mods_norm=mods, shift_idx=3, w_router=w_r)
            a = _swiglu_experts(h, moe_w_gu[j].astype(BF16), comb, tm=tm, tn=512)
            w_down = moe_w_down[j].reshape(-1, d).astype(BF16)
            f = _matmul(a, w_down, tm=tm, tn=512, name="moe_down")

        if i + 1 < depth:
            mods_next = _ada(cond, ada_w, ada_b3, i + 1)
            xs, h = stream(xs, y=f, g_res=g[3:4], mods_res=mods, gate_idx=5,
                           g_norm=norm_g[i + 1, 0:1], mods_norm=mods_next, shift_idx=0)
            mods = mods_next
        else:
            (xs,) = stream(xs, y=f, g_res=g[3:4], mods_res=mods, gate_idx=5)

    return xs.reshape(n_batch, tokens, d)[:, ctx_len:]
```

```python
import functools
import math

import jax
import jax.numpy as jnp
from jax import lax
from jax.experimental import pallas as pl
from jax.experimental.pallas import tpu as pltpu

F32 = jnp.float32
BF16 = jnp.bfloat16

GRID_W = 64
ROPE_BASE = 10000.0
EPS = 1e-6
N_MOD = 6

MLA_HEADS = 32
MLA_Q_RANK = 1024
MLA_KV_RANK = 512
MLA_NOPE_DIM = 128
MLA_ROPE_DIM = 64
MLA_V_DIM = 128
MLA_QK_PAD = 256
MLA_SCALE = (MLA_NOPE_DIM + MLA_ROPE_DIM) ** -0.5

DIFF_HEAD_DIM = 128

N_EXPERTS = 8

LANES = 128
MOD_ROWS = 8
TOKEN_TILE = 768
ATTN_Q_TILE = 512
ATTN_STREAMS = 4
ONES_ROWS = 16
OVERSHOOT_LIMIT = 60.0
VMEM_LIMIT = 56 * 1024 * 1024
LOG2E = math.log2(math.e)


def _params(*semantics):
    return pltpu.CompilerParams(dimension_semantics=semantics, vmem_limit_bytes=VMEM_LIMIT)


def _rms(x, g):
    return x * lax.rsqrt(jnp.mean(x * x, axis=-1, keepdims=True) + EPS) * g


def _swap_halves(x, half):
    lane = lax.broadcasted_iota(jnp.int32, x.shape, 1)
    first = (lane % (2 * half)) < half
    return jnp.where(first, pltpu.roll(x, LANES - half, 1), pltpu.roll(x, half, 1))


def _rope(x, cos, sin, half):
    return x * cos + _swap_halves(x, half) * sin


def _ada_kernel(c_ref, w_ref, b_ref, o_ref):
    c = c_ref[...]
    s = (c * jax.nn.sigmoid(c)).astype(BF16)
    o_ref[...] = jnp.dot(s, w_ref[...].astype(BF16), preferred_element_type=F32) + b_ref[...]


def _ada(cond, ada_w, ada_b3, layer):
    d = cond.shape[1]
    n = ada_w.shape[2]
    tn = 512
    return pl.pallas_call(
        _ada_kernel,
        grid=(n // tn,),
        in_specs=[
            pl.BlockSpec((MOD_ROWS, d), lambda j: (0, 0)),
            pl.BlockSpec((None, d, tn), lambda j: (layer, 0, j)),
            pl.BlockSpec((None, 1, tn), lambda j: (layer, 0, j)),
        ],
        out_specs=pl.BlockSpec((MOD_ROWS, tn), lambda j: (0, j)),
        out_shape=jax.ShapeDtypeStruct((MOD_ROWS, n), F32),
        compiler_params=_params("arbitrary"),
        name="ada",
    )(cond, ada_w, ada_b3)


def _stream_kernel(*refs, n_batch, d, resid, norm, router, gate_idx, shift_idx):
    refs = list(refs)
    x_ref = refs.pop(0)
    if resid:
        y_ref, gres_ref, mres_ref = refs.pop(0), refs.pop(0), refs.pop(0)
    if norm:
        gnorm_ref, mnorm_ref = refs.pop(0), refs.pop(0)
    if router:
        wr_ref = refs.pop(0)
    if resid:
        xo_ref = refs.pop(0)
    if norm:
        h_ref = refs.pop(0)
    if router:
        comb_ref = refs.pop(0)

    row = jnp.where(pl.program_id(1) == 0, n_batch, pl.program_id(0))

    def mod(ref, k):
        return ref[pl.ds(row, 1), pl.ds(k * d, d)]

    x = x_ref[...]
    if resid:
        x = x + mod(mres_ref, gate_idx) * _rms(y_ref[...].astype(F32), gres_ref[...])
        xo_ref[...] = x
    if norm:
        h = _rms(x, gnorm_ref[...]) * (1.0 + mod(mnorm_ref, shift_idx + 1)) + mod(mnorm_ref, shift_idx)
        h_ref[...] = h.astype(BF16)
    if router:
        logits = jnp.dot(h, wr_ref[...], precision=lax.Precision.HIGHEST, preferred_element_type=F32)
        lane = lax.broadcasted_iota(jnp.int32, logits.shape, 1)
        neg = jnp.float32(-jnp.inf)
        lg = jnp.where(lane < N_EXPERTS, logits, neg)
        m1 = jnp.max(lg, axis=1, keepdims=True)
        i1 = jnp.min(jnp.where(lg == m1, lane, LANES), axis=1, keepdims=True)
        lg2 = jnp.where(lane == i1, neg, lg)
        m2 = jnp.max(lg2, axis=1, keepdims=True)
        i2 = jnp.min(jnp.where(lg2 == m2, lane, LANES), axis=1, keepdims=True)
        e2 = jnp.exp(m2 - m1)
        w1 = 1.0 / (1.0 + e2)
        w2 = e2 / (1.0 + e2)
        comb_ref[...] = jnp.where(lane == i1, w1, jnp.where(lane == i2, w2, 0.0))


def _stream(x, *, n_batch, tile, y=None, g_res=None, mods_res=None, gate_idx=0,
            g_norm=None, mods_norm=None, shift_idx=0, w_router=None):
    n, d = x.shape
    resid, norm, router = y is not None, g_norm is not None, w_router is not None
    tpb = n // n_batch // tile
    row_spec = pl.BlockSpec((tile, d), lambda b, t: (b * tpb + t, 0))
    vec_spec = pl.BlockSpec((1, d), lambda b, t: (0, 0))
    mod_spec = pl.BlockSpec((MOD_ROWS, N_MOD * d), lambda b, t: (0, 0))
    args, in_specs, out_shape, out_specs = [x], [row_spec], [], []
    if resid:
        args += [y, g_res, mods_res]
        in_specs += [row_spec, vec_spec, mod_spec]
        out_shape.append(jax.ShapeDtypeStruct((n, d), F32))
        out_specs.append(row_spec)
    if norm:
        args += [g_norm, mods_norm]
        in_specs += [vec_spec, mod_spec]
        out_shape.append(jax.ShapeDtypeStruct((n, d), BF16))
        out_specs.append(row_spec)
    if router:
        args.append(w_router)
        in_specs.append(pl.BlockSpec((d, LANES), lambda b, t: (0, 0)))
        out_shape.append(jax.ShapeDtypeStruct((n, LANES), F32))
        out_specs.append(pl.BlockSpec((tile, LANES), lambda b, t: (b * tpb + t, 0)))
    kernel = functools.partial(_stream_kernel, n_batch=n_batch, d=d, resid=resid, norm=norm,
                               router=router, gate_idx=gate_idx, shift_idx=shift_idx)
    return pl.pallas_call(
        kernel,
        grid=(n_batch, tpb),
        in_specs=in_specs,
        out_specs=out_specs,
        out_shape=out_shape,
        input_output_aliases={0: 0} if resid else {},
        compiler_params=_params("arbitrary", "arbitrary"),
        name="stream",
    )(*args)


def _mm_kernel(x_ref, w_ref, o_ref):
    o_ref[...] = jnp.dot(x_ref[...], w_ref[...], preferred_element_type=F32).astype(o_ref.dtype)


def _matmul(x, w, *, tm, tn, out_dtype=BF16, name="mm"):
    m, k = x.shape
    n = w.shape[1]
    tn = min(tn, n)
    return pl.pallas_call(
        _mm_kernel,
        grid=(m // tm, n // tn),
        in_specs=[pl.BlockSpec((tm, k), lambda i, j: (i, 0)),
                  pl.BlockSpec((k, tn), lambda i, j: (0, j))],
        out_specs=pl.BlockSpec((tm, tn), lambda i, j: (i, j)),
        out_shape=jax.ShapeDtypeStruct((m, n), out_dtype),
        compiler_params=_params("arbitrary", "arbitrary"),
        name=name,
    )(x, w)


def _swiglu_kernel(*refs, tiles_per_expert):
    if tiles_per_expert:
        x_ref, wg_ref, wu_ref, comb_ref, o_ref = refs
    else:
        x_ref, wg_ref, wu_ref, o_ref = refs
    x = x_ref[...]
    g = jnp.dot(x, wg_ref[...], preferred_element_type=F32)
    u = jnp.dot(x, wu_ref[...], preferred_element_type=F32)
    a = g * jax.nn.sigmoid(g) * u
    if tiles_per_expert:
        e = pl.program_id(1) // tiles_per_expert
        comb = comb_ref[...]
        lane = lax.broadcasted_iota(jnp.int32, comb.shape, 1)
        a = a * jnp.sum(jnp.where(lane == e, comb, 0.0), axis=1, keepdims=True)
    o_ref[...] = a.astype(o_ref.dtype)


def _swiglu_dense(x, w_gu, *, tm, tn):
    m, k = x.shape
    f = w_gu.shape[1] // 2
    nj = f // tn
    return pl.pallas_call(
        functools.partial(_swiglu_kernel, tiles_per_expert=0),
        grid=(m // tm, nj),
        in_specs=[pl.BlockSpec((tm, k), lambda i, j: (i, 0)),
                  pl.BlockSpec((k, tn), lambda i, j: (0, j)),
                  pl.BlockSpec((k, tn), lambda i, j: (0, nj + j))],
        out_specs=pl.BlockSpec((tm, tn), lambda i, j: (i, j)),
        out_shape=jax.ShapeDtypeStruct((m, f), BF16),
        compiler_params=_params("arbitrary", "arbitrary"),
        name="swiglu",
    )(x, w_gu, w_gu)


def _swiglu_experts(x, w_gu, comb, *, tm, tn):
    m, k = x.shape
    n_exp, _, f2 = w_gu.shape
    f = f2 // 2
    tpe = f // tn
    return pl.pallas_call(
        functools.partial(_swiglu_kernel, tiles_per_expert=tpe),
        grid=(m // tm, n_exp * tpe),
        in_specs=[pl.BlockSpec((tm, k), lambda i, j: (i, 0)),
                  pl.BlockSpec((None, k, tn), lambda i, j: (j // tpe, 0, j % tpe)),
                  pl.BlockSpec((None, k, tn), lambda i, j: (j // tpe, 0, tpe + j % tpe)),
                  pl.BlockSpec((tm, LANES), lambda i, j: (i, 0))],
        out_specs=pl.BlockSpec((tm, tn), lambda i, j: (i, j)),
        out_shape=jax.ShapeDtypeStruct((m, n_exp * f), BF16),
        compiler_params=_params("arbitrary", "arbitrary"),
        name="swiglu_experts",
    )(x, w_gu, w_gu, comb)


def _rope_mm_kernel(x_ref, w_ref, cos_ref, sin_ref, o_ref, *, half, rope_every, scale):
    acc = jnp.dot(x_ref[...], w_ref[...], preferred_element_type=F32)
    cos, sin = cos_ref[...], sin_ref[...]
    for g in range(acc.shape[1] // LANES):
        blk = acc[:, g * LANES:(g + 1) * LANES]
        if g % rope_every == rope_every - 1:
            blk = _rope(blk, cos, sin, half)
        o_ref[:, g * LANES:(g + 1) * LANES] = (blk * scale).astype(o_ref.dtype)


def _rope_matmul(x, w, cos, sin, *, tm, tn, half, rope_every, scale, name):
    m, k = x.shape
    n = w.shape[1]
    tn = min(tn, n)
    tpb = cos.shape[0] // tm
    return pl.pallas_call(
        functools.partial(_rope_mm_kernel, half=half, rope_every=rope_every, scale=scale),
        grid=(m // tm, n // tn),
        in_specs=[pl.BlockSpec((tm, k), lambda i, j: (i, 0)),
                  pl.BlockSpec((k, tn), lambda i, j: (0, j)),
                  pl.BlockSpec((tm, LANES), lambda i, j: (i % tpb, 0)),
                  pl.BlockSpec((tm, LANES), lambda i, j: (i % tpb, 0))],
        out_specs=pl.BlockSpec((tm, tn), lambda i, j: (i, j)),
        out_shape=jax.ShapeDtypeStruct((m, n), BF16),
        compiler_params=_params("arbitrary", "arbitrary"),
        name=name,
    )(x, w, cos, sin)


def _vt_kernel(w_ref, x_ref, o_ref, *, dv):
    res = lax.dot_general(w_ref[...], x_ref[...], (((1,), (1,)), ((), ())),
                          preferred_element_type=F32).astype(o_ref.dtype)
    dvp = dv + ONES_ROWS
    ones = jnp.ones((ONES_ROWS, res.shape[1]), o_ref.dtype)
    for h in range(res.shape[0] // dv):
        o_ref[h * dvp:h * dvp + dv, :] = res[h * dv:(h + 1) * dv]
        o_ref[h * dvp + dv:(h + 1) * dvp, :] = ones


def _vt_matmul(w_t, x, *, chunk, tv, dv):
    d_all, k = w_t.shape
    n = x.shape[0]
    tv = min(tv, d_all)
    tvp = tv // dv * (dv + ONES_ROWS)
    return pl.pallas_call(
        functools.partial(_vt_kernel, dv=dv),
        grid=(n // chunk, d_all // tv),
        in_specs=[pl.BlockSpec((tv, k), lambda c, j: (j, 0)),
                  pl.BlockSpec((chunk, k), lambda c, j: (c, 0))],
        out_specs=pl.BlockSpec((None, tvp, chunk), lambda c, j: (c, j, 0)),
        out_shape=jax.ShapeDtypeStruct((n // chunk, d_all // dv * (dv + ONES_ROWS), chunk), BF16),
        compiler_params=_params("arbitrary", "arbitrary"),
        name="vt",
    )(w_t, x)


def _mla_down_kernel(x_ref, w_ref, qn_ref, kvn_ref, cos_ref, sin_ref, cq_ref, cn_ref, kpe_ref):
    acc = jnp.dot(x_ref[...], w_ref[...], preferred_element_type=F32)
    r0, r1 = MLA_Q_RANK, MLA_Q_RANK + MLA_KV_RANK
    cq_ref[...] = _rms(acc[:, :r0], qn_ref[...]).astype(BF16)
    cn_ref[...] = _rms(acc[:, r0:r1], kvn_ref[...]).astype(BF16)
    kpe_ref[...] = _rope(acc[:, r1:], cos_ref[...], sin_ref[...], MLA_ROPE_DIM // 4).astype(BF16)


def _mla_down(h, w_cat, q_norm, kv_norm, cos, sin, *, tm):
    m, k = h.shape
    n = w_cat.shape[1]
    tpb = cos.shape[0] // tm
    row = lambda width: pl.BlockSpec((tm, width), lambda i: (i, 0))
    tab = pl.BlockSpec((tm, LANES), lambda i: (i % tpb, 0))
    return pl.pallas_call(
        _mla_down_kernel,
        grid=(m // tm,),
        in_specs=[row(k),
                  pl.BlockSpec((k, n), lambda i: (0, 0)),
                  pl.BlockSpec((1, MLA_Q_RANK), lambda i: (0, 0)),
                  pl.BlockSpec((1, MLA_KV_RANK), lambda i: (0, 0)),
                  tab, tab],
        out_specs=[row(MLA_Q_RANK), row(MLA_KV_RANK), row(LANES)],
        out_shape=[jax.ShapeDtypeStruct((m, MLA_Q_RANK), BF16),
                   jax.ShapeDtypeStruct((m, MLA_KV_RANK), BF16),
                   jax.ShapeDtypeStruct((m, LANES), BF16)],
        compiler_params=_params("arbitrary"),
        name="mla_down",
    )(h, w_cat, q_norm, kv_norm, cos, sin)


def _mla_kup_kernel(c_ref, w_ref, kpe_ref, o_ref):
    acc = jnp.dot(c_ref[...], w_ref[...], preferred_element_type=F32).astype(BF16)
    kpe = kpe_ref[...]
    for h in range(MLA_HEADS):
        o_ref[:, h * MLA_QK_PAD:h * MLA_QK_PAD + LANES] = acc[:, h * LANES:(h + 1) * LANES]
        o_ref[:, h * MLA_QK_PAD + LANES:(h + 1) * MLA_QK_PAD] = kpe


def _mla_kup(cn, w_uk, kpe, *, tm):
    m, k = cn.shape
    n_out = MLA_HEADS * MLA_QK_PAD
    return pl.pallas_call(
        _mla_kup_kernel,
        grid=(m // tm,),
        in_specs=[pl.BlockSpec((tm, k), lambda i: (i, 0)),
                  pl.BlockSpec((k, MLA_HEADS * MLA_NOPE_DIM), lambda i: (0, 0)),
                  pl.BlockSpec((tm, LANES), lambda i: (i, 0))],
        out_specs=pl.BlockSpec((tm, n_out), lambda i: (i, 0)),
        out_shape=jax.ShapeDtypeStruct((m, n_out), BF16),
        compiler_params=_params("arbitrary"),
        name="mla_kup",
    )(cn, w_uk, kpe)


def _fold_rows(op, s):
    r = s.shape[0]
    while r > 8 and r % 16 == 0:
        r //= 2
        s = op(s[:r], s[r:])
    return s


def _scores(k, q):
    return lax.dot_general(k, q, (((1,), (1,)), ((), ())), preferred_element_type=F32)


def _attend(q_ref, k_ref, vt_ref, o_ref, acc_ref, finish, *, n_maps, n_chunks, chunk, ctx_len, tq):
    tokens, dqk = q_ref.shape
    dv = vt_ref.shape[1] - ONES_ROWS
    dm = dqk // n_maps
    n_lanes = ATTN_STREAMS * n_maps

    def colmax(s):
        return jnp.max(_fold_rows(jnp.maximum, s), axis=0, keepdims=True)

    def split(acc):
        return acc[dv:dv + 1], acc[:dv]

    def pv(v_t, p):
        return jnp.dot(v_t, p.astype(BF16), preferred_element_type=F32)

    parts = []
    for a in range(n_maps):
        s = _scores(k_ref[0:ctx_len, a * dm:(a + 1) * dm], q_ref[0:ctx_len, a * dm:(a + 1) * dm])
        parts.append(split(pv(vt_ref[0, :, 0:ctx_len], jnp.exp2(s - colmax(s)))))
    o_ref[0:ctx_len, :] = finish(parts)

    def q_body(i, _):
        offs = [pl.multiple_of(ctx_len + (i * ATTN_STREAMS + j) * tq, math.gcd(ctx_len, tq))
                for j in range(ATTN_STREAMS)]

        def lane_scores(c, w):
            j, a = divmod(w, n_maps)
            k = k_ref[pl.ds(pl.multiple_of(c * chunk, chunk), chunk), a * dm:(a + 1) * dm]
            return _scores(k, q_ref[pl.ds(offs[j], tq), a * dm:(a + 1) * dm])

        def write_out():
            for j, off in enumerate(offs):
                o_ref[pl.ds(off, tq), :] = finish([split(acc_ref[j * n_maps + a]) for a in range(n_maps)])

        shifts = []
        for w in range(n_lanes):
            s = lane_scores(0, w)
            m = colmax(s)
            acc_ref[w] = pv(vt_ref[0], jnp.exp2(s - m))
            shifts.append(m)

        def fast_body(c, carry):
            shifts, over = carry
            v_t = vt_ref[c]
            out_shifts, out_over = [], []
            for w in range(n_lanes):
                s = lane_scores(c, w)
                m_old = shifts[w]
                cm = colmax(s)
                m_new = jnp.maximum(m_old, cm)
                acc_ref[w] = (acc_ref[w] + pv(v_t, jnp.exp2(s - m_old))) * jnp.exp2(m_old - m_new)
                out_shifts.append(m_new)
                out_over.append(jnp.maximum(over[w], cm - m_old))
            return tuple(out_shifts), tuple(out_over)

        zero = jnp.zeros((1, tq), F32)
        carry = (tuple(shifts), (zero,) * n_lanes)
        n_pairs = (n_chunks - 1) // 2
        carry = lax.fori_loop(0, n_pairs, lambda t, cr: fast_body(2 * t + 2, fast_body(2 * t + 1, cr)), carry)
        if (n_chunks - 1) % 2:
            carry = fast_body(n_chunks - 1, carry)
        _, over = carry
        write_out()

        worst = over[0]
        for w in range(1, n_lanes):
            worst = jnp.maximum(worst, over[w])

        @pl.when(jnp.max(worst) > OVERSHOOT_LIMIT)
        def _():
            def safe_body(c, ms):
                v_t = vt_ref[c]
                out = []
                for w in range(n_lanes):
                    s = lane_scores(c, w)
                    m_new = jnp.maximum(ms[w], colmax(s))
                    acc_ref[w] = acc_ref[w] * jnp.exp2(ms[w] - m_new) + pv(v_t, jnp.exp2(s - m_new))
                    out.append(m_new)
                return tuple(out)

            acc_ref[...] = jnp.zeros(acc_ref.shape, F32)
            lax.fori_loop(0, n_chunks, safe_body, (jnp.full((1, tq), -jnp.inf, F32),) * n_lanes)
            write_out()

        return 0

    lax.fori_loop(0, (tokens - ctx_len) // (tq * ATTN_STREAMS), q_body, 0)


def _mla_attn_kernel(q_ref, k_ref, vt_ref, o_ref, acc_ref, **kw):
    def finish(parts):
        ((l, acc),) = parts
        return (acc / l).T.astype(o_ref.dtype)

    _attend(q_ref, k_ref, vt_ref, o_ref, acc_ref, finish, n_maps=1, **kw)


def _diff_attn_kernel(lam_ref, sub_ref, q_ref, k_ref, vt_ref, o_ref, acc_ref, *, lambda_init, **kw):
    def finish(parts):
        (l1, a1), (l2, a2) = parts
        lp = lam_ref[...]
        lam = (jnp.exp(jnp.sum(lp[0:1] * lp[1:2], keepdims=True))
               - jnp.exp(jnp.sum(lp[2:3] * lp[3:4], keepdims=True)) + lambda_init)
        o = (a1 / l1 - lam * (a2 / l2)).T
        return (_rms(o, sub_ref[...]) * (1.0 - lambda_init)).astype(o_ref.dtype)

    _attend(q_ref, k_ref, vt_ref, o_ref, acc_ref, finish, n_maps=2, **kw)


def _attention(kernel, q, k, vt, *, n_batch, n_heads, n_maps, dqk, dv, ctx_len, tq, chunk, extra=(), name):
    n = q.shape[0]
    tokens = n // n_batch
    n_chunks = tokens // chunk
    n_lanes = ATTN_STREAMS * n_maps
    dvp = dv + ONES_ROWS
    extra_specs = [pl.BlockSpec(e.shape, lambda b, h: (0, 0)) for e in extra]
    return pl.pallas_call(
        functools.partial(kernel, n_chunks=n_chunks, chunk=chunk, ctx_len=ctx_len, tq=tq),
        grid=(n_batch, n_heads),
        in_specs=extra_specs + [
            pl.BlockSpec((tokens, dqk), lambda b, h: (b, h)),
            pl.BlockSpec((tokens, dqk), lambda b, h: (b, h)),
            pl.BlockSpec((n_chunks, dvp, chunk), lambda b, h: (b, h, 0)),
        ],
        out_specs=pl.BlockSpec((tokens, dv), lambda b, h: (b, h)),
        out_shape=jax.ShapeDtypeStruct((n, n_heads * dv), BF16),
        scratch_shapes=[pltpu.VMEM((n_lanes, dvp, tq), F32)],
        compiler_params=_params("arbitrary", "arbitrary"),
        name=name,
    )(*extra, q, k, vt)


def _rope_tables(ctx_len, seq, rot_dim):
    half = rot_dim // 2
    n_freq = half // 2
    t = jnp.arange(seq, dtype=jnp.int32)
    pos_row = (t // GRID_W).astype(F32)
    pos_col = (t % GRID_W).astype(F32)
    inv = ROPE_BASE ** (-jnp.arange(0, half, 2, dtype=F32) / half)
    lane = jnp.arange(LANES)
    freq = inv[(lane % half) % n_freq]
    pos = jnp.where((lane // half)[None, :] == 0, pos_row[:, None], pos_col[:, None])
    ang = pos * freq[None, :]
    live = (lane < rot_dim)[None, :]
    sign = jnp.where((lane % half) < n_freq, -1.0, 1.0)[None, :]
    cos = jnp.where(live, jnp.cos(ang), 1.0)
    sin = jnp.where(live, jnp.sin(ang) * sign, 0.0)
    cos = jnp.concatenate([jnp.ones((ctx_len, LANES), F32), cos], axis=0)
    sin = jnp.concatenate([jnp.zeros((ctx_len, LANES), F32), sin], axis=0)
    return cos, sin


def kernel(x, c, ctx, c_ctx, ada_w, ada_b, norm_g, mla_w_dq, mla_q_norm, mla_w_uq, mla_w_dkv, mla_kv_norm, mla_w_ukv, mla_w_o, diff_w_qkv, diff_lambda, diff_subln, diff_w_o, ffn_w_gu, ffn_w_down, moe_router, moe_w_gu, moe_w_down):
    n_batch, seq, d = x.shape
    ctx_len = ctx.shape[1]
    depth = ada_w.shape[0]
    tokens = ctx_len + seq
    tm = TOKEN_TILE
    assert tokens % tm == 0 and tokens % ctx_len == 0 and ctx_len % LANES == 0 and ctx_len <= tm
    assert n_batch < MOD_ROWS and seq % (ATTN_Q_TILE * ATTN_STREAMS) == 0
    diff_heads = d // (2 * DIFF_HEAD_DIM)

    xs = jnp.concatenate([ctx, x], axis=1).reshape(n_batch * tokens, d)
    cond = jnp.zeros((MOD_ROWS, d), F32).at[:n_batch].set(c).at[n_batch].set(c_ctx)
    ada_b3 = ada_b.reshape(depth, 1, N_MOD * d)
    stream = functools.partial(_stream, n_batch=n_batch, tile=ctx_len)

    cos_d, sin_d = _rope_tables(ctx_len, seq, DIFF_HEAD_DIM)
    cos_m, sin_m = _rope_tables(ctx_len, seq, MLA_ROPE_DIM)

    mods = _ada(cond, ada_w, ada_b3, 0)
    (h,) = stream(xs, g_norm=norm_g[0, 0:1], mods_norm=mods, shift_idx=0)

    for i in range(depth):
        j = i // 2
        g = norm_g[i]
        if i % 2 == 0:
            w_cat = jnp.concatenate(
                [mla_w_dq[j], mla_w_dkv[j], jnp.zeros((d, LANES - MLA_ROPE_DIM), F32)], axis=1).astype(BF16)
            w_uq = mla_w_uq[j].reshape(MLA_Q_RANK, MLA_HEADS, MLA_NOPE_DIM + MLA_ROPE_DIM)
            w_uq = jnp.pad(w_uq, ((0, 0), (0, 0), (0, MLA_QK_PAD - w_uq.shape[2])))
            w_uq = w_uq.reshape(MLA_Q_RANK, MLA_HEADS * MLA_QK_PAD).astype(BF16)
            w_ukv = mla_w_ukv[j].reshape(MLA_KV_RANK, MLA_HEADS, MLA_NOPE_DIM + MLA_V_DIM)
            w_uk = w_ukv[:, :, :MLA_NOPE_DIM].reshape(MLA_KV_RANK, -1).astype(BF16)
            w_uv_t = w_ukv[:, :, MLA_NOPE_DIM:].reshape(MLA_KV_RANK, -1).T.astype(BF16)

            cq, cn, kpe = _mla_down(h, w_cat, mla_q_norm[j][None], mla_kv_norm[j][None], cos_m, sin_m, tm=tm)
            q = _rope_matmul(cq, w_uq, cos_m, sin_m, tm=tm, tn=2048, half=MLA_ROPE_DIM // 4,
                             rope_every=2, scale=MLA_SCALE * LOG2E, name="mla_q")
            k = _mla_kup(cn, w_uk, kpe, tm=tm)
            vt = _vt_matmul(w_uv_t, cn, chunk=tm, tv=w_uv_t.shape[0], dv=MLA_V_DIM)
            o = _attention(_mla_attn_kernel, q, k, vt, n_batch=n_batch, n_heads=MLA_HEADS, n_maps=1,
                           dqk=MLA_QK_PAD, dv=MLA_V_DIM, ctx_len=ctx_len, tq=ATTN_Q_TILE, chunk=tm,
                           name="mla_attn")
            y = _matmul(o, mla_w_o[j].astype(BF16), tm=tm, tn=1024, name="mla_o")
        else:
            lambda_init = 0.8 - 0.6 * math.exp(-0.3 * i)
            w_q = diff_w_qkv[j][:, :d].astype(BF16)
            w_k = diff_w_qkv[j][:, d:2 * d].astype(BF16)
            w_v_t = diff_w_qkv[j][:, 2 * d:].T.astype(BF16)
            scale = DIFF_HEAD_DIM ** -0.5 * LOG2E
            q = _rope_matmul(h, w_q, cos_d, sin_d, tm=tm, tn=1024, half=DIFF_HEAD_DIM // 4,
                             rope_every=1, scale=scale, name="diff_q")
            k = _rope_matmul(h, w_k, cos_d, sin_d, tm=tm, tn=1024, half=DIFF_HEAD_DIM // 4,
                             rope_every=1, scale=1.0, name="diff_k")
            vt = _vt_matmul(w_v_t, h, chunk=tm, tv=1024, dv=2 * DIFF_HEAD_DIM)
            o = _attention(functools.partial(_diff_attn_kernel, lambda_init=lambda_init), q, k, vt,
                           n_batch=n_batch, n_heads=diff_heads, n_maps=2, dqk=2 * DIFF_HEAD_DIM,
                           dv=2 * DIFF_HEAD_DIM, ctx_len=ctx_len, tq=ATTN_Q_TILE, chunk=tm,
                           extra=(diff_lambda[j], diff_subln[j][None]), name="diff_attn")
            y = _matmul(o, diff_w_o[j].astype(BF16), tm=tm, tn=1024, name="diff_o")

        if i % 2 == 0:
            xs, h = stream(xs, y=y, g_res=g[1:2], mods_res=mods, gate_idx=2,
                           g_norm=g[2:3], mods_norm=mods, shift_idx=3)
            a = _swiglu_dense(h, ffn_w_gu[j].astype(BF16), tm=tm, tn=512)
            f = _matmul(a, ffn_w_down[j].astype(BF16), tm=tm, tn=1024, name="ffn_down")
        else:
            w_r = jnp.pad(moe_router[j], ((0, 0), (0, LANES - N_EXPERTS)))
            xs, h, comb = stream(xs, y=y, g_res=g[1:2], mods_res=mods, gate_idx=2,
                                 g_norm=g[2:3], mods_norm=mods, shift_idx=3, w_router=w_r)
            a = _swiglu_experts(h, moe_w_gu[j].astype(BF16), comb, tm=tm, tn=512)
            w_down = moe_w_down[j].reshape(-1, d).astype(BF16)
            f = _matmul(a, w_down, tm=tm, tn=512, name="moe_down")

        if i + 1 < depth:
            mods_next = _ada(cond, ada_w, ada_b3, i + 1)
            xs, h = stream(xs, y=f, g_res=g[3:4], mods_res=mods, gate_idx=5,
                           g_norm=norm_g[i + 1, 0:1], mods_norm=mods_next, shift_idx=0)
            mods = mods_next
        else:
            (xs,) = stream(xs, y=f, g_res=g[3:4], mods_res=mods, gate_idx=5)

    return xs.reshape(n_batch, tokens, d)[:, ctx_len:]
```

```python
import functools
import math

import jax
import jax.numpy as jnp
from jax import lax
from jax.experimental import pallas as pl
from jax.experimental.pallas import tpu as pltpu

F32 = jnp.float32
BF16 = jnp.bfloat16

GRID_W = 64
ROPE_BASE = 10000.0
EPS = 1e-6
N_MOD = 6

MLA_HEADS = 32
MLA_Q_RANK = 1024
MLA_KV_RANK = 512
MLA_NOPE_DIM = 128
MLA_ROPE_DIM = 64
MLA_V_DIM = 128
MLA_QK_PAD = 256
MLA_SCALE = (MLA_NOPE_DIM + MLA_ROPE_DIM) ** -0.5

DIFF_HEAD_DIM = 128

N_EXPERTS = 8

LANES = 128
MOD_ROWS = 8
TOKEN_TILE = 768
MOE_TILE = 512
GATHER_TILE = 256
ATTN_Q_TILE = 512
ATTN_STREAMS = 4
ONES_ROWS = 16
OVERSHOOT_LIMIT = 60.0
VMEM_LIMIT = 56 * 1024 * 1024
LOG2E = math.log2(math.e)


def _params(*semantics):
    return pltpu.CompilerParams(dimension_semantics=semantics, vmem_limit_bytes=VMEM_LIMIT)


def _rms(x, g):
    return x * lax.rsqrt(jnp.mean(x * x, axis=-1, keepdims=True) + EPS) * g


def _swap_halves(x, half):
    lane = lax.broadcasted_iota(jnp.int32, x.shape, 1)
    first = (lane % (2 * half)) < half
    return jnp.where(first, pltpu.roll(x, LANES - half, 1), pltpu.roll(x, half, 1))


def _rope(x, cos, sin, half):
    return x * cos + _swap_halves(x, half) * sin


def _ada_kernel(c_ref, w_ref, b_ref, o_ref):
    c = c_ref[...]
    s = (c * jax.nn.sigmoid(c)).astype(BF16)
    o_ref[...] = jnp.dot(s, w_ref[...].astype(BF16), preferred_element_type=F32) + b_ref[...]


def _ada(cond, ada_w, ada_b3, layer):
    d = cond.shape[1]
    n = ada_w.shape[2]
    tn = 512
    return pl.pallas_call(
        _ada_kernel,
        grid=(n // tn,),
        in_specs=[
            pl.BlockSpec((MOD_ROWS, d), lambda j: (0, 0)),
            pl.BlockSpec((None, d, tn), lambda j: (layer, 0, j)),
            pl.BlockSpec((None, 1, tn), lambda j: (layer, 0, j)),
        ],
        out_specs=pl.BlockSpec((MOD_ROWS, tn), lambda j: (0, j)),
        out_shape=jax.ShapeDtypeStruct((MOD_ROWS, n), F32),
        compiler_params=_params("arbitrary"),
        name="ada",
    )(cond, ada_w, ada_b3)


def _stream_kernel(*refs, n_batch, d, resid, norm, router, gate_idx, shift_idx):
    refs = list(refs)
    x_ref = refs.pop(0)
    if resid:
        y_ref, gres_ref, mres_ref = refs.pop(0), refs.pop(0), refs.pop(0)
    if norm:
        gnorm_ref, mnorm_ref = refs.pop(0), refs.pop(0)
    if router:
        wr_ref = refs.pop(0)
    if resid:
        xo_ref = refs.pop(0)
    if norm:
        h_ref = refs.pop(0)
    if router:
        comb_ref = refs.pop(0)

    row = jnp.where(pl.program_id(1) == 0, n_batch, pl.program_id(0))

    def mod(ref, k):
        return ref[pl.ds(row, 1), pl.ds(k * d, d)]

    x = x_ref[...]
    if resid:
        x = x + mod(mres_ref, gate_idx) * _rms(y_ref[...].astype(F32), gres_ref[...])
        xo_ref[...] = x
    if norm:
        h = _rms(x, gnorm_ref[...]) * (1.0 + mod(mnorm_ref, shift_idx + 1)) + mod(mnorm_ref, shift_idx)
        h_ref[...] = h.astype(h_ref.dtype)
    if router:
        logits = jnp.dot(h, wr_ref[...], precision=lax.Precision.HIGHEST, preferred_element_type=F32)
        lane = lax.broadcasted_iota(jnp.int32, logits.shape, 1)
        neg = jnp.float32(-jnp.inf)
        lg = jnp.where(lane < N_EXPERTS, logits, neg)
        m1 = jnp.max(lg, axis=1, keepdims=True)
        i1 = jnp.min(jnp.where(lg == m1, lane, LANES), axis=1, keepdims=True)
        lg2 = jnp.where(lane == i1, neg, lg)
        m2 = jnp.max(lg2, axis=1, keepdims=True)
        i2 = jnp.min(jnp.where(lg2 == m2, lane, LANES), axis=1, keepdims=True)
        e2 = jnp.exp(m2 - m1)
        w1 = 1.0 / (1.0 + e2)
        w2 = e2 / (1.0 + e2)
        route = jnp.where(lane == 0, i1.astype(F32), jnp.where(lane == 1, i2.astype(F32),
                          jnp.where(lane == 2, w1, jnp.where(lane == 3, w2, 0.0))))
        comb_ref[...] = route


def _stream(x, *, n_batch, tile, y=None, g_res=None, mods_res=None, gate_idx=0,
            g_norm=None, mods_norm=None, shift_idx=0, w_router=None):
    n, d = x.shape
    resid, norm, router = y is not None, g_norm is not None, w_router is not None
    tpb = n // n_batch // tile
    row_spec = pl.BlockSpec((tile, d), lambda b, t: (b * tpb + t, 0))
    vec_spec = pl.BlockSpec((1, d), lambda b, t: (0, 0))
    mod_spec = pl.BlockSpec((MOD_ROWS, N_MOD * d), lambda b, t: (0, 0))
    args, in_specs, out_shape, out_specs = [x], [row_spec], [], []
    if resid:
        args += [y, g_res, mods_res]
        in_specs += [row_spec, vec_spec, mod_spec]
        out_shape.append(jax.ShapeDtypeStruct((n, d), F32))
        out_specs.append(row_spec)
    if norm:
        args += [g_norm, mods_norm]
        in_specs += [vec_spec, mod_spec]
        out_shape.append(jax.ShapeDtypeStruct((n, d), F32 if router else BF16))
        out_specs.append(row_spec)
    if router:
        args.append(w_router)
        in_specs.append(pl.BlockSpec((d, LANES), lambda b, t: (0, 0)))
        out_shape.append(jax.ShapeDtypeStruct((n, LANES), F32))
        out_specs.append(pl.BlockSpec((tile, LANES), lambda b, t: (b * tpb + t, 0)))
    kernel = functools.partial(_stream_kernel, n_batch=n_batch, d=d, resid=resid, norm=norm,
                               router=router, gate_idx=gate_idx, shift_idx=shift_idx)
    return pl.pallas_call(
        kernel,
        grid=(n_batch, tpb),
        in_specs=in_specs,
        out_specs=out_specs,
        out_shape=out_shape,
        input_output_aliases={0: 0} if resid else {},
        compiler_params=_params("arbitrary", "arbitrary"),
        name="stream",
    )(*args)


def _mm_kernel(x_ref, w_ref, o_ref):
    o_ref[...] = jnp.dot(x_ref[...], w_ref[...], preferred_element_type=F32).astype(o_ref.dtype)


def _matmul(x, w, *, tm, tn, out_dtype=BF16, name="mm"):
    m, k = x.shape
    n = w.shape[1]
    tn = min(tn, n)
    return pl.pallas_call(
        _mm_kernel,
        grid=(m // tm, n // tn),
        in_specs=[pl.BlockSpec((tm, k), lambda i, j: (i, 0)),
                  pl.BlockSpec((k, tn), lambda i, j: (0, j))],
        out_specs=pl.BlockSpec((tm, tn), lambda i, j: (i, j)),
        out_shape=jax.ShapeDtypeStruct((m, n), out_dtype),
        compiler_params=_params("arbitrary", "arbitrary"),
        name=name,
    )(x, w)


def _swiglu_kernel(x_ref, wg_ref, wu_ref, o_ref):
    x = x_ref[...]
    g = jnp.dot(x, wg_ref[...], preferred_element_type=F32)
    u = jnp.dot(x, wu_ref[...], preferred_element_type=F32)
    o_ref[...] = (g * jax.nn.sigmoid(g) * u).astype(o_ref.dtype)


def _swiglu_dense(x, w_gu, *, tm, tn):
    m, k = x.shape
    f = w_gu.shape[1] // 2
    nj = f // tn
    return pl.pallas_call(
        _swiglu_kernel,
        grid=(m // tm, nj),
        in_specs=[pl.BlockSpec((tm, k), lambda i, j: (i, 0)),
                  pl.BlockSpec((k, tn), lambda i, j: (0, j)),
                  pl.BlockSpec((k, tn), lambda i, j: (0, nj + j))],
        out_specs=pl.BlockSpec((tm, tn), lambda i, j: (i, j)),
        out_shape=jax.ShapeDtypeStruct((m, f), BF16),
        compiler_params=_params("arbitrary", "arbitrary"),
        name="swiglu",
    )(x, w_gu, w_gu)


def _moe_up_kernel(te_ref, x_ref, wg_ref, wu_ref, o_ref):
    _swiglu_kernel(x_ref, wg_ref, wu_ref, o_ref)


def _moe_down_kernel(te_ref, a_ref, w_ref, o_ref):
    _mm_kernel(a_ref, w_ref, o_ref)


def _moe_up(x_g, w_gu, tile_expert, *, tn):
    r, k = x_g.shape
    f = w_gu.shape[2] // 2
    nj = f // tn
    return pl.pallas_call(
        _moe_up_kernel,
        grid_spec=pltpu.PrefetchScalarGridSpec(
            num_scalar_prefetch=1,
            grid=(r // MOE_TILE, nj),
            in_specs=[pl.BlockSpec((MOE_TILE, k), lambda i, j, te: (i, 0)),
                      pl.BlockSpec((None, k, tn), lambda i, j, te: (te[i], 0, j)),
                      pl.BlockSpec((None, k, tn), lambda i, j, te: (te[i], 0, nj + j))],
            out_specs=pl.BlockSpec((MOE_TILE, tn), lambda i, j, te: (i, j))),
        out_shape=jax.ShapeDtypeStruct((r, f), BF16),
        compiler_params=_params("arbitrary", "arbitrary"),
        name="moe_up",
    )(tile_expert, x_g, w_gu, w_gu)


def _moe_down(a_g, w_down, tile_expert):
    r, f = a_g.shape
    d = w_down.shape[2]
    return pl.pallas_call(
        _moe_down_kernel,
        grid_spec=pltpu.PrefetchScalarGridSpec(
            num_scalar_prefetch=1,
            grid=(r // MOE_TILE,),
            in_specs=[pl.BlockSpec((MOE_TILE, f), lambda i, te: (i, 0)),
                      pl.BlockSpec((None, f, d), lambda i, te: (te[i], 0, 0))],
            out_specs=pl.BlockSpec((MOE_TILE, d), lambda i, te: (i, 0))),
        out_shape=jax.ShapeDtypeStruct((r, d), F32),
        compiler_params=_params("arbitrary"),
        name="moe_down",
    )(tile_expert, a_g, w_down)


def _gather_kernel(idx_ref, *refs, n_src, weighted):
    if weighted:
        route_ref, src_hbm, o_ref, buf, sem = refs
    else:
        src_hbm, o_ref, buf, sem = refs
    tile = o_ref.shape[0]

    def row_copy(s, r, src_row):
        return pltpu.make_async_copy(src_hbm.at[pl.ds(src_row, 1)], buf.at[s, pl.ds(r, 1)], sem)

    def start(r, carry):
        for s in range(n_src):
            row_copy(s, r, idx_ref[s, r]).start()
        return carry

    def wait(r, carry):
        for s in range(n_src):
            row_copy(s, r, 0).wait()
        return carry

    lax.fori_loop(0, tile, start, 0)
    lax.fori_loop(0, tile, wait, 0)
    if weighted:
        route = route_ref[...]
        out = buf[0] * route[:, 2:3]
        for s in range(1, n_src):
            out = out + buf[s] * route[:, 2 + s:3 + s]
    else:
        out = buf[0]
    o_ref[...] = out.astype(o_ref.dtype)


def _gather_rows(idx, src, *, out_dtype, route=None, name):
    n_src, r = idx.shape
    d = src.shape[1]
    in_specs = [pl.BlockSpec((n_src, GATHER_TILE), lambda i: (0, i), memory_space=pltpu.SMEM)]
    args = [idx]
    if route is not None:
        in_specs.append(pl.BlockSpec((GATHER_TILE, LANES), lambda i: (i, 0)))
        args.append(route)
    in_specs.append(pl.BlockSpec(memory_space=pl.ANY))
    args.append(src)
    return pl.pallas_call(
        functools.partial(_gather_kernel, n_src=n_src, weighted=route is not None),
        grid=(r // GATHER_TILE,),
        in_specs=in_specs,
        out_specs=pl.BlockSpec((GATHER_TILE, d), lambda i: (i, 0)),
        out_shape=jax.ShapeDtypeStruct((r, d), out_dtype),
        scratch_shapes=[pltpu.VMEM((n_src, GATHER_TILE, d), src.dtype), pltpu.SemaphoreType.DMA],
        compiler_params=_params("arbitrary"),
        name=name,
    )(*args)


def _route_tables(route, n_rows):
    n = route.shape[0]
    expert = jnp.concatenate([route[:, 0], route[:, 1]]).astype(jnp.int32)
    token = jnp.concatenate([jnp.arange(n, dtype=jnp.int32)] * 2)
    onehot = (expert[:, None] == jnp.arange(N_EXPERTS, dtype=jnp.int32)[None, :]).astype(jnp.int32)
    rank = jnp.sum((jnp.cumsum(onehot, axis=0) - onehot) * onehot, axis=1)
    counts = jnp.sum(onehot, axis=0)
    padded = (counts + MOE_TILE - 1) // MOE_TILE * MOE_TILE
    ends = jnp.cumsum(padded)
    pos = (ends - padded)[expert] + rank
    src = jnp.zeros((n_rows,), jnp.int32).at[pos].set(token)
    tile_start = jnp.arange(n_rows // MOE_TILE, dtype=jnp.int32) * MOE_TILE
    tile_expert = jnp.minimum(jnp.searchsorted(ends, tile_start, side="right"), N_EXPERTS - 1).astype(jnp.int32)
    return src[None, :], tile_expert, pos.reshape(2, n)


def _rope_mm_kernel(x_ref, w_ref, cos_ref, sin_ref, o_ref, *, half, rope_every, scale):
    acc = jnp.dot(x_ref[...], w_ref[...], preferred_element_type=F32)
    cos, sin = cos_ref[...], sin_ref[...]
    for g in range(acc.shape[1] // LANES):
        blk = acc[:, g * LANES:(g + 1) * LANES]
        if g % rope_every == rope_every - 1:
            blk = _rope(blk, cos, sin, half)
        o_ref[:, g * LANES:(g + 1) * LANES] = (blk * scale).astype(o_ref.dtype)


def _rope_matmul(x, w, cos, sin, *, tm, tn, half, rope_every, scale, name):
    m, k = x.shape
    n = w.shape[1]
    tn = min(tn, n)
    tpb = cos.shape[0] // tm
    return pl.pallas_call(
        functools.partial(_rope_mm_kernel, half=half, rope_every=rope_every, scale=scale),
        grid=(m // tm, n // tn),
        in_specs=[pl.BlockSpec((tm, k), lambda i, j: (i, 0)),
                  pl.BlockSpec((k, tn), lambda i, j: (0, j)),
                  pl.BlockSpec((tm, LANES), lambda i, j: (i % tpb, 0)),
                  pl.BlockSpec((tm, LANES), lambda i, j: (i % tpb, 0))],
        out_specs=pl.BlockSpec((tm, tn), lambda i, j: (i, j)),
        out_shape=jax.ShapeDtypeStruct((m, n), BF16),
        compiler_params=_params("arbitrary", "arbitrary"),
        name=name,
    )(x, w, cos, sin)


def _vt_kernel(w_ref, x_ref, o_ref, *, dv):
    res = lax.dot_general(w_ref[...], x_ref[...], (((1,), (1,)), ((), ())),
                          preferred_element_type=F32).astype(o_ref.dtype)
    dvp = dv + ONES_ROWS
    ones = jnp.ones((ONES_ROWS, res.shape[1]), o_ref.dtype)
    for h in range(res.shape[0] // dv):
        o_ref[h * dvp:h * dvp + dv, :] = res[h * dv:(h + 1) * dv]
        o_ref[h * dvp + dv:(h + 1) * dvp, :] = ones


def _vt_matmul(w_t, x, *, chunk, tv, dv):
    d_all, k = w_t.shape
    n = x.shape[0]
    tv = min(tv, d_all)
    tvp = tv // dv * (dv + ONES_ROWS)
    return pl.pallas_call(
        functools.partial(_vt_kernel, dv=dv),
        grid=(n // chunk, d_all // tv),
        in_specs=[pl.BlockSpec((tv, k), lambda c, j: (j, 0)),
                  pl.BlockSpec((chunk, k), lambda c, j: (c, 0))],
        out_specs=pl.BlockSpec((None, tvp, chunk), lambda c, j: (c, j, 0)),
        out_shape=jax.ShapeDtypeStruct((n // chunk, d_all // dv * (dv + ONES_ROWS), chunk), BF16),
        compiler_params=_params("arbitrary", "arbitrary"),
        name="vt",
    )(w_t, x)


def _mla_down_kernel(x_ref, w_ref, qn_ref, kvn_ref, cos_ref, sin_ref, cq_ref, cn_ref, kpe_ref):
    acc = jnp.dot(x_ref[...], w_ref[...], preferred_element_type=F32)
    r0, r1 = MLA_Q_RANK, MLA_Q_RANK + MLA_KV_RANK
    cq_ref[...] = _rms(acc[:, :r0], qn_ref[...]).astype(BF16)
    cn_ref[...] = _rms(acc[:, r0:r1], kvn_ref[...]).astype(BF16)
    kpe_ref[...] = _rope(acc[:, r1:], cos_ref[...], sin_ref[...], MLA_ROPE_DIM // 4).astype(BF16)


def _mla_down(h, w_cat, q_norm, kv_norm, cos, sin, *, tm):
    m, k = h.shape
    n = w_cat.shape[1]
    tpb = cos.shape[0] // tm
    row = lambda width: pl.BlockSpec((tm, width), lambda i: (i, 0))
    tab = pl.BlockSpec((tm, LANES), lambda i: (i % tpb, 0))
    return pl.pallas_call(
        _mla_down_kernel,
        grid=(m // tm,),
        in_specs=[row(k),
                  pl.BlockSpec((k, n), lambda i: (0, 0)),
                  pl.BlockSpec((1, MLA_Q_RANK), lambda i: (0, 0)),
                  pl.BlockSpec((1, MLA_KV_RANK), lambda i: (0, 0)),
                  tab, tab],
        out_specs=[row(MLA_Q_RANK), row(MLA_KV_RANK), row(LANES)],
        out_shape=[jax.ShapeDtypeStruct((m, MLA_Q_RANK), BF16),
                   jax.ShapeDtypeStruct((m, MLA_KV_RANK), BF16),
                   jax.ShapeDtypeStruct((m, LANES), BF16)],
        compiler_params=_params("arbitrary"),
        name="mla_down",
    )(h, w_cat, q_norm, kv_norm, cos, sin)


def _mla_kup_kernel(c_ref, w_ref, kpe_ref, o_ref):
    acc = jnp.dot(c_ref[...], w_ref[...], preferred_element_type=F32).astype(BF16)
    kpe = kpe_ref[...]
    for h in range(MLA_HEADS):
        o_ref[:, h * MLA_QK_PAD:h * MLA_QK_PAD + LANES] = acc[:, h * LANES:(h + 1) * LANES]
        o_ref[:, h * MLA_QK_PAD + LANES:(h + 1) * MLA_QK_PAD] = kpe


def _mla_kup(cn, w_uk, kpe, *, tm):
    m, k = cn.shape
    n_out = MLA_HEADS * MLA_QK_PAD
    return pl.pallas_call(
        _mla_kup_kernel,
        grid=(m // tm,),
        in_specs=[pl.BlockSpec((tm, k), lambda i: (i, 0)),
                  pl.BlockSpec((k, MLA_HEADS * MLA_NOPE_DIM), lambda i: (0, 0)),
                  pl.BlockSpec((tm, LANES), lambda i: (i, 0))],
        out_specs=pl.BlockSpec((tm, n_out), lambda i: (i, 0)),
        out_shape=jax.ShapeDtypeStruct((m, n_out), BF16),
        compiler_params=_params("arbitrary"),
        name="mla_kup",
    )(cn, w_uk, kpe)


def _fold_rows(op, s):
    r = s.shape[0]
    while r > 8 and r % 16 == 0:
        r //= 2
        s = op(s[:r], s[r:])
    return s


def _scores(k, q):
    return lax.dot_general(k, q, (((1,), (1,)), ((), ())), preferred_element_type=F32)


def _attend(q_ref, k_ref, vt_ref, o_ref, acc_ref, finish, *, n_maps, n_chunks, chunk, ctx_len, tq):
    tokens, dqk = q_ref.shape
    dv = vt_ref.shape[1] - ONES_ROWS
    dm = dqk // n_maps
    n_lanes = ATTN_STREAMS * n_maps

    def colmax(s):
        return jnp.max(_fold_rows(jnp.maximum, s), axis=0, keepdims=True)

    def split(acc):
        return acc[dv:dv + 1], acc[:dv]

    def pv(v_t, p):
        return jnp.dot(v_t, p.astype(BF16), preferred_element_type=F32)

    parts = []
    for a in range(n_maps):
        s = _scores(k_ref[0:ctx_len, a * dm:(a + 1) * dm], q_ref[0:ctx_len, a * dm:(a + 1) * dm])
        parts.append(split(pv(vt_ref[0, :, 0:ctx_len], jnp.exp2(s - colmax(s)))))
    o_ref[0:ctx_len, :] = finish(parts)

    def q_body(i, _):
        offs = [pl.multiple_of(ctx_len + (i * ATTN_STREAMS + j) * tq, math.gcd(ctx_len, tq))
                for j in range(ATTN_STREAMS)]

        def lane_scores(c, w):
            j, a = divmod(w, n_maps)
            k = k_ref[pl.ds(pl.multiple_of(c * chunk, chunk), chunk), a * dm:(a + 1) * dm]
            return _scores(k, q_ref[pl.ds(offs[j], tq), a * dm:(a + 1) * dm])

        def write_out():
            for j, off in enumerate(offs):
                o_ref[pl.ds(off, tq), :] = finish([split(acc_ref[j * n_maps + a]) for a in range(n_maps)])

        shifts = []
        for w in range(n_lanes):
            s = lane_scores(0, w)
            m = colmax(s)
            acc_ref[w] = pv(vt_ref[0], jnp.exp2(s - m))
            shifts.append(m)

        def fast_body(c, carry):
            shifts, over = carry
            v_t = vt_ref[c]
            out_shifts, out_over = [], []
            for w in range(n_lanes):
                s = lane_scores(c, w)
                m_old = shifts[w]
                cm = colmax(s)
                m_new = jnp.maximum(m_old, cm)
                acc_ref[w] = (acc_ref[w] + pv(v_t, jnp.exp2(s - m_old))) * jnp.exp2(m_old - m_new)
                out_shifts.append(m_new)
                out_over.append(jnp.maximum(over[w], cm - m_old))
            return tuple(out_shifts), tuple(out_over)

        zero = jnp.zeros((1, tq), F32)
        carry = (tuple(shifts), (zero,) * n_lanes)
        n_pairs = (n_chunks - 1) // 2
        carry = lax.fori_loop(0, n_pairs, lambda t, cr: fast_body(2 * t + 2, fast_body(2 * t + 1, cr)), carry)
        if (n_chunks - 1) % 2:
            carry = fast_body(n_chunks - 1, carry)
        _, over = carry
        write_out()

        worst = over[0]
        for w in range(1, n_lanes):
            worst = jnp.maximum(worst, over[w])

        @pl.when(jnp.max(worst) > OVERSHOOT_LIMIT)
        def _():
            def safe_body(c, ms):
                v_t = vt_ref[c]
                out = []
                for w in range(n_lanes):
                    s = lane_scores(c, w)
                    m_new = jnp.maximum(ms[w], colmax(s))
                    acc_ref[w] = acc_ref[w] * jnp.exp2(ms[w] - m_new) + pv(v_t, jnp.exp2(s - m_new))
                    out.append(m_new)
                return tuple(out)

            acc_ref[...] = jnp.zeros(acc_ref.shape, F32)
            lax.fori_loop(0, n_chunks, safe_body, (jnp.full((1, tq), -jnp.inf, F32),) * n_lanes)
            write_out()

        return 0

    lax.fori_loop(0, (tokens - ctx_len) // (tq * ATTN_STREAMS), q_body, 0)


def _mla_attn_kernel(q_ref, k_ref, vt_ref, o_ref, acc_ref, **kw):
    def finish(parts):
        ((l, acc),) = parts
        return (acc / l).T.astype(o_ref.dtype)

    _attend(q_ref, k_ref, vt_ref, o_ref, acc_ref, finish, n_maps=1, **kw)


def _diff_attn_kernel(lam_ref, sub_ref, q_ref, k_ref, vt_ref, o_ref, acc_ref, *, lambda_init, **kw):
    def finish(parts):
        (l1, a1), (l2, a2) = parts
        lp = lam_ref[...]
        lam = (jnp.exp(jnp.sum(lp[0:1] * lp[1:2], keepdims=True))
               - jnp.exp(jnp.sum(lp[2:3] * lp[3:4], keepdims=True)) + lambda_init)
        o = (a1 / l1 - lam * (a2 / l2)).T
        return (_rms(o, sub_ref[...]) * (1.0 - lambda_init)).astype(o_ref.dtype)

    _attend(q_ref, k_ref, vt_ref, o_ref, acc_ref, finish, n_maps=2, **kw)


def _attention(kernel, q, k, vt, *, n_batch, n_heads, n_maps, dqk, dv, ctx_len, tq, chunk, extra=(), name):
    n = q.shape[0]
    tokens = n // n_batch
    n_chunks = tokens // chunk
    n_lanes = ATTN_STREAMS * n_maps
    dvp = dv + ONES_ROWS
    extra_specs = [pl.BlockSpec(e.shape, lambda b, h: (0, 0)) for e in extra]
    return pl.pallas_call(
        functools.partial(kernel, n_chunks=n_chunks, chunk=chunk, ctx_len=ctx_len, tq=tq),
        grid=(n_batch, n_heads),
        in_specs=extra_specs + [
            pl.BlockSpec((tokens, dqk), lambda b, h: (b, h)),
            pl.BlockSpec((tokens, dqk), lambda b, h: (b, h)),
            pl.BlockSpec((n_chunks, dvp, chunk), lambda b, h: (b, h, 0)),
        ],
        out_specs=pl.BlockSpec((tokens, dv), lambda b, h: (b, h)),
        out_shape=jax.ShapeDtypeStruct((n, n_heads * dv), BF16),
        scratch_shapes=[pltpu.VMEM((n_lanes, dvp, tq), F32)],
        compiler_params=_params("arbitrary", "arbitrary"),
        name=name,
    )(*extra, q, k, vt)


def _rope_tables(ctx_len, seq, rot_dim):
    half = rot_dim // 2
    n_freq = half // 2
    t = jnp.arange(seq, dtype=jnp.int32)
    pos_row = (t // GRID_W).astype(F32)
    pos_col = (t % GRID_W).astype(F32)
    inv = ROPE_BASE ** (-jnp.arange(0, half, 2, dtype=F32) / half)
    lane = jnp.arange(LANES)
    freq = inv[(lane % half) % n_freq]
    pos = jnp.where((lane // half)[None, :] == 0, pos_row[:, None], pos_col[:, None])
    ang = pos * freq[None, :]
    live = (lane < rot_dim)[None, :]
    sign = jnp.where((lane % half) < n_freq, -1.0, 1.0)[None, :]
    cos = jnp.where(live, jnp.cos(ang), 1.0)
    sin = jnp.where(live, jnp.sin(ang) * sign, 0.0)
    cos = jnp.concatenate([jnp.ones((ctx_len, LANES), F32), cos], axis=0)
    sin = jnp.concatenate([jnp.zeros((ctx_len, LANES), F32), sin], axis=0)
    return cos, sin


def kernel(x, c, ctx, c_ctx, ada_w, ada_b, norm_g, mla_w_dq, mla_q_norm, mla_w_uq, mla_w_dkv, mla_kv_norm, mla_w_ukv, mla_w_o, diff_w_qkv, diff_lambda, diff_subln, diff_w_o, ffn_w_gu, ffn_w_down, moe_router, moe_w_gu, moe_w_down):
    n_batch, seq, d = x.shape
    ctx_len = ctx.shape[1]
    depth = ada_w.shape[0]
    tokens = ctx_len + seq
    tm = TOKEN_TILE
    assert tokens % tm == 0 and tokens % ctx_len == 0 and ctx_len % LANES == 0 and ctx_len <= tm
    assert n_batch < MOD_ROWS and seq % (ATTN_Q_TILE * ATTN_STREAMS) == 0
    diff_heads = d // (2 * DIFF_HEAD_DIM)

    xs = jnp.concatenate([ctx, x], axis=1).reshape(n_batch * tokens, d)
    cond = jnp.zeros((MOD_ROWS, d), F32).at[:n_batch].set(c).at[n_batch].set(c_ctx)
    ada_b3 = ada_b.reshape(depth, 1, N_MOD * d)
    stream = functools.partial(_stream, n_batch=n_batch, tile=ctx_len)

    cos_d, sin_d = _rope_tables(ctx_len, seq, DIFF_HEAD_DIM)
    cos_m, sin_m = _rope_tables(ctx_len, seq, MLA_ROPE_DIM)

    mods = _ada(cond, ada_w, ada_b3, 0)
    (h,) = stream(xs, g_norm=norm_g[0, 0:1], mods_norm=mods, shift_idx=0)

    for i in range(depth):
        j = i // 2
        g = norm_g[i]
        if i % 2 == 0:
            w_cat = jnp.concatenate(
                [mla_w_dq[j], mla_w_dkv[j], jnp.zeros((d, LANES - MLA_ROPE_DIM), F32)], axis=1).astype(BF16)
            w_uq = mla_w_uq[j].reshape(MLA_Q_RANK, MLA_HEADS, MLA_NOPE_DIM + MLA_ROPE_DIM)
            w_uq = jnp.pad(w_uq, ((0, 0), (0, 0), (0, MLA_QK_PAD - w_uq.shape[2])))
            w_uq = w_uq.reshape(MLA_Q_RANK, MLA_HEADS * MLA_QK_PAD).astype(BF16)
            w_ukv = mla_w_ukv[j].reshape(MLA_KV_RANK, MLA_HEADS, MLA_NOPE_DIM + MLA_V_DIM)
            w_uk = w_ukv[:, :, :MLA_NOPE_DIM].reshape(MLA_KV_RANK, -1).astype(BF16)
            w_uv_t = w_ukv[:, :, MLA_NOPE_DIM:].reshape(MLA_KV_RANK, -1).T.astype(BF16)

            cq, cn, kpe = _mla_down(h, w_cat, mla_q_norm[j][None], mla_kv_norm[j][None], cos_m, sin_m, tm=tm)
            q = _rope_matmul(cq, w_uq, cos_m, sin_m, tm=tm, tn=2048, half=MLA_ROPE_DIM // 4,
                             rope_every=2, scale=MLA_SCALE * LOG2E, name="mla_q")
            k = _mla_kup(cn, w_uk, kpe, tm=tm)
            vt = _vt_matmul(w_uv_t, cn, chunk=tm, tv=w_uv_t.shape[0], dv=MLA_V_DIM)
            o = _attention(_mla_attn_kernel, q, k, vt, n_batch=n_batch, n_heads=MLA_HEADS, n_maps=1,
                           dqk=MLA_QK_PAD, dv=MLA_V_DIM, ctx_len=ctx_len, tq=ATTN_Q_TILE, chunk=tm,
                           name="mla_attn")
            y = _matmul(o, mla_w_o[j].astype(BF16), tm=tm, tn=1024, name="mla_o")
        else:
            lambda_init = 0.8 - 0.6 * math.exp(-0.3 * i)
            w_q = diff_w_qkv[j][:, :d].astype(BF16)
            w_k = diff_w_qkv[j][:, d:2 * d].astype(BF16)
            w_v_t = diff_w_qkv[j][:, 2 * d:].T.astype(BF16)
            scale = DIFF_HEAD_DIM ** -0.5 * LOG2E
            q = _rope_matmul(h, w_q, cos_d, sin_d, tm=tm, tn=1024, half=DIFF_HEAD_DIM // 4,
                             rope_every=1, scale=scale, name="diff_q")
            k = _rope_matmul(h, w_k, cos_d, sin_d, tm=tm, tn=1024, half=DIFF_HEAD_DIM // 4,
                             rope_every=1, scale=1.0, name="diff_k")
            vt = _vt_matmul(w_v_t, h, chunk=tm, tv=1024, dv=2 * DIFF_HEAD_DIM)
            o = _attention(functools.partial(_diff_attn_kernel, lambda_init=lambda_init), q, k, vt,
                           n_batch=n_batch, n_heads=diff_heads, n_maps=2, dqk=2 * DIFF_HEAD_DIM,
                           dv=2 * DIFF_HEAD_DIM, ctx_len=ctx_len, tq=ATTN_Q_TILE, chunk=tm,
                           extra=(diff_lambda[j], diff_subln[j][None]), name="diff_attn")
            y = _matmul(o, diff_w_o[j].astype(BF16), tm=tm, tn=1024, name="diff_o")

        if i % 2 == 0:
            xs, h = stream(xs, y=y, g_res=g[1:2], mods_res=mods, gate_idx=2,
                           g_norm=g[2:3], mods_norm=mods, shift_idx=3)
            a = _swiglu_dense(h, ffn_w_gu[j].astype(BF16), tm=tm, tn=512)
            f = _matmul(a, ffn_w_down[j].astype(BF16), tm=tm, tn=1024, name="ffn_down")
        else:
            w_r = jnp.pad(moe_router[j], ((0, 0), (0, LANES - N_EXPERTS)))
            xs, h32, route = stream(xs, y=y, g_res=g[1:2], mods_res=mods, gate_idx=2,
                                    g_norm=g[2:3], mods_norm=mods, shift_idx=3, w_router=w_r)
            n_rows = 2 * xs.shape[0] + N_EXPERTS * MOE_TILE
            src, tile_expert, pos = _route_tables(route, n_rows)
            x_g = _gather_rows(src, h32, out_dtype=BF16, name="moe_dispatch")
            a_g = _moe_up(x_g, moe_w_gu[j].astype(BF16), tile_expert, tn=512)
            y_g = _moe_down(a_g, moe_w_down[j].astype(BF16), tile_expert)
            f = _gather_rows(pos, y_g, out_dtype=BF16, route=route, name="moe_combine")

        if i + 1 < depth:
            mods_next = _ada(cond, ada_w, ada_b3, i + 1)
            xs, h = stream(xs, y=f, g_res=g[3:4], mods_res=mods, gate_idx=5,
                           g_norm=norm_g[i + 1, 0:1], mods_norm=mods_next, shift_idx=0)
            mods = mods_next
        else:
            (xs,) = stream(xs, y=f, g_res=g[3:4], mods_res=mods, gate_idx=5)

    return xs.reshape(n_batch, tokens, d)[:, ctx_len:]
```

```python
import functools
import math

import jax
import jax.numpy as jnp
from jax import lax
from jax.experimental import pallas as pl
from jax.experimental.pallas import tpu as pltpu

F32 = jnp.float32
BF16 = jnp.bfloat16

GRID_W = 64
ROPE_BASE = 10000.0
EPS = 1e-6
N_MOD = 6

MLA_HEADS = 32
MLA_Q_RANK = 1024
MLA_KV_RANK = 512
MLA_NOPE_DIM = 128
MLA_ROPE_DIM = 64
MLA_V_DIM = 128
MLA_QK_PAD = 256
MLA_SCALE = (MLA_NOPE_DIM + MLA_ROPE_DIM) ** -0.5

DIFF_HEAD_DIM = 128

N_EXPERTS = 8

LANES = 128
MOD_ROWS = 8
TOKEN_TILE = 768
MOE_TILE = 512
GATHER_TILE = 512
ATTN_Q_TILE = 512
ATTN_STREAMS = 4
ONES_ROWS = 16
OVERSHOOT_LIMIT = 60.0
VMEM_LIMIT = 56 * 1024 * 1024
LOG2E = math.log2(math.e)


def _params(*semantics):
    return pltpu.CompilerParams(dimension_semantics=semantics, vmem_limit_bytes=VMEM_LIMIT)


def _rms(x, g):
    return x * lax.rsqrt(jnp.mean(x * x, axis=-1, keepdims=True) + EPS) * g


def _swap_halves(x, half):
    lane = lax.broadcasted_iota(jnp.int32, x.shape, 1)
    first = (lane % (2 * half)) < half
    return jnp.where(first, pltpu.roll(x, LANES - half, 1), pltpu.roll(x, half, 1))


def _rope(x, cos, sin, half):
    return x * cos + _swap_halves(x, half) * sin


def _ada_kernel(c_ref, w_ref, b_ref, o_ref):
    c = c_ref[...]
    s = (c * jax.nn.sigmoid(c)).astype(BF16)
    o_ref[...] = jnp.dot(s, w_ref[...].astype(BF16), preferred_element_type=F32) + b_ref[...]


def _ada(cond, ada_w, ada_b3, layer):
    d = cond.shape[1]
    n = ada_w.shape[2]
    tn = 512
    return pl.pallas_call(
        _ada_kernel,
        grid=(n // tn,),
        in_specs=[
            pl.BlockSpec((MOD_ROWS, d), lambda j: (0, 0)),
            pl.BlockSpec((None, d, tn), lambda j: (layer, 0, j)),
            pl.BlockSpec((None, 1, tn), lambda j: (layer, 0, j)),
        ],
        out_specs=pl.BlockSpec((MOD_ROWS, tn), lambda j: (0, j)),
        out_shape=jax.ShapeDtypeStruct((MOD_ROWS, n), F32),
        compiler_params=_params("arbitrary"),
        name="ada",
    )(cond, ada_w, ada_b3)


def _stream_kernel(*refs, n_batch, d, resid, norm, router, gate_idx, shift_idx, latent_only):
    refs = list(refs)
    x_ref = refs.pop(0)
    if resid:
        y_ref, gres_ref, mres_ref = refs.pop(0), refs.pop(0), refs.pop(0)
    if norm:
        gnorm_ref, mnorm_ref = refs.pop(0), refs.pop(0)
    if router:
        wr_ref = refs.pop(0)
    if resid:
        xo_ref = refs.pop(0)
    if norm:
        h_ref = refs.pop(0)
    if router:
        comb_ref = refs.pop(0)

    row = pl.program_id(0) if latent_only else jnp.where(pl.program_id(1) == 0, n_batch, pl.program_id(0))

    def mod(ref, k):
        return ref[pl.ds(row, 1), pl.ds(k * d, d)]

    x = x_ref[...]
    if resid:
        x = x + mod(mres_ref, gate_idx) * _rms(y_ref[...].astype(F32), gres_ref[...])
        xo_ref[...] = x
    if norm:
        h = _rms(x, gnorm_ref[...]) * (1.0 + mod(mnorm_ref, shift_idx + 1)) + mod(mnorm_ref, shift_idx)
        h_ref[...] = h.astype(h_ref.dtype)
    if router:
        logits = jnp.dot(h, wr_ref[...], precision=lax.Precision.HIGHEST, preferred_element_type=F32)
        lane = lax.broadcasted_iota(jnp.int32, logits.shape, 1)
        neg = jnp.float32(-jnp.inf)
        lg = jnp.where(lane < N_EXPERTS, logits, neg)
        m1 = jnp.max(lg, axis=1, keepdims=True)
        i1 = jnp.min(jnp.where(lg == m1, lane, LANES), axis=1, keepdims=True)
        lg2 = jnp.where(lane == i1, neg, lg)
        m2 = jnp.max(lg2, axis=1, keepdims=True)
        i2 = jnp.min(jnp.where(lg2 == m2, lane, LANES), axis=1, keepdims=True)
        e2 = jnp.exp(m2 - m1)
        w1 = 1.0 / (1.0 + e2)
        w2 = e2 / (1.0 + e2)
        route = jnp.where(lane == 0, i1.astype(F32), jnp.where(lane == 1, i2.astype(F32),
                          jnp.where(lane == 2, w1, jnp.where(lane == 3, w2, 0.0))))
        comb_ref[...] = route


def _stream(x, *, n_batch, tile, y=None, g_res=None, mods_res=None, gate_idx=0,
            g_norm=None, mods_norm=None, shift_idx=0, w_router=None, latent_only=False):
    n, d = x.shape
    resid, norm, router = y is not None, g_norm is not None, w_router is not None
    tpb = n // n_batch // tile
    if latent_only:
        assert resid and not norm
        return pl.pallas_call(
            functools.partial(_stream_kernel, n_batch=n_batch, d=d, resid=True, norm=False, router=False,
                              gate_idx=gate_idx, shift_idx=shift_idx, latent_only=True),
            grid=(n_batch, tpb - 1),
            in_specs=[pl.BlockSpec((tile, d), lambda b, t: (b * tpb + t + 1, 0)),
                      pl.BlockSpec((tile, d), lambda b, t: (b * tpb + t + 1, 0)),
                      pl.BlockSpec((1, d), lambda b, t: (0, 0)),
                      pl.BlockSpec((MOD_ROWS, N_MOD * d), lambda b, t: (0, 0))],
            out_specs=[pl.BlockSpec((tile, d), lambda b, t: (b * (tpb - 1) + t, 0))],
            out_shape=[jax.ShapeDtypeStruct((n - n_batch * tile, d), F32)],
            compiler_params=_params("arbitrary", "arbitrary"),
            name="stream_out",
        )(x, y, g_res, mods_res)
    row_spec = pl.BlockSpec((tile, d), lambda b, t: (b * tpb + t, 0))
    vec_spec = pl.BlockSpec((1, d), lambda b, t: (0, 0))
    mod_spec = pl.BlockSpec((MOD_ROWS, N_MOD * d), lambda b, t: (0, 0))
    args, in_specs, out_shape, out_specs = [x], [row_spec], [], []
    if resid:
        args += [y, g_res, mods_res]
        in_specs += [row_spec, vec_spec, mod_spec]
        out_shape.append(jax.ShapeDtypeStruct((n, d), F32))
        out_specs.append(row_spec)
    if norm:
        args += [g_norm, mods_norm]
        in_specs += [vec_spec, mod_spec]
        out_shape.append(jax.ShapeDtypeStruct((n, d), F32 if router else BF16))
        out_specs.append(row_spec)
    if router:
        args.append(w_router)
        in_specs.append(pl.BlockSpec((d, LANES), lambda b, t: (0, 0)))
        out_shape.append(jax.ShapeDtypeStruct((n, LANES), F32))
        out_specs.append(pl.BlockSpec((tile, LANES), lambda b, t: (b * tpb + t, 0)))
    kernel = functools.partial(_stream_kernel, n_batch=n_batch, d=d, resid=resid, norm=norm,
                               router=router, gate_idx=gate_idx, shift_idx=shift_idx, latent_only=False)
    return pl.pallas_call(
        kernel,
        grid=(n_batch, tpb),
        in_specs=in_specs,
        out_specs=out_specs,
        out_shape=out_shape,
        input_output_aliases={0: 0} if resid else {},
        compiler_params=_params("arbitrary", "arbitrary"),
        name="stream",
    )(*args)


def _mm_kernel(x_ref, w_ref, o_ref):
    o_ref[...] = jnp.dot(x_ref[...], w_ref[...], preferred_element_type=F32).astype(o_ref.dtype)


def _matmul(x, w, *, tm, tn, out_dtype=BF16, name="mm"):
    m, k = x.shape
    n = w.shape[1]
    tn = min(tn, n)
    return pl.pallas_call(
        _mm_kernel,
        grid=(m // tm, n // tn),
        in_specs=[pl.BlockSpec((tm, k), lambda i, j: (i, 0)),
                  pl.BlockSpec((k, tn), lambda i, j: (0, j))],
        out_specs=pl.BlockSpec((tm, tn), lambda i, j: (i, j)),
        out_shape=jax.ShapeDtypeStruct((m, n), out_dtype),
        compiler_params=_params("arbitrary", "arbitrary"),
        name=name,
    )(x, w)


def _swiglu_kernel(x_ref, wg_ref, wu_ref, o_ref):
    x = x_ref[...]
    g = jnp.dot(x, wg_ref[...], preferred_element_type=F32)
    u = jnp.dot(x, wu_ref[...], preferred_element_type=F32)
    o_ref[...] = (g * jax.nn.sigmoid(g) * u).astype(o_ref.dtype)


def _swiglu_dense(x, w_gu, *, tm, tn):
    m, k = x.shape
    f = w_gu.shape[1] // 2
    nj = f // tn
    return pl.pallas_call(
        _swiglu_kernel,
        grid=(m // tm, nj),
        in_specs=[pl.BlockSpec((tm, k), lambda i, j: (i, 0)),
                  pl.BlockSpec((k, tn), lambda i, j: (0, j)),
                  pl.BlockSpec((k, tn), lambda i, j: (0, nj + j))],
        out_specs=pl.BlockSpec((tm, tn), lambda i, j: (i, j)),
        out_shape=jax.ShapeDtypeStruct((m, f), BF16),
        compiler_params=_params("arbitrary", "arbitrary"),
        name="swiglu",
    )(x, w_gu, w_gu)


def _moe_up_kernel(te_ref, x_ref, wg_ref, wu_ref, o_ref):
    _swiglu_kernel(x_ref, wg_ref, wu_ref, o_ref)


def _moe_down_kernel(te_ref, a_ref, w_ref, o_ref):
    _mm_kernel(a_ref, w_ref, o_ref)


def _moe_up(x_g, w_gu, tile_expert, *, tn):
    r, k = x_g.shape
    f = w_gu.shape[2] // 2
    nj = f // tn
    return pl.pallas_call(
        _moe_up_kernel,
        grid_spec=pltpu.PrefetchScalarGridSpec(
            num_scalar_prefetch=1,
            grid=(r // MOE_TILE, nj),
            in_specs=[pl.BlockSpec((MOE_TILE, k), lambda i, j, te: (i, 0)),
                      pl.BlockSpec((None, k, tn), lambda i, j, te: (te[i], 0, j)),
                      pl.BlockSpec((None, k, tn), lambda i, j, te: (te[i], 0, nj + j))],
            out_specs=pl.BlockSpec((MOE_TILE, tn), lambda i, j, te: (i, j))),
        out_shape=jax.ShapeDtypeStruct((r, f), BF16),
        compiler_params=_params("arbitrary", "arbitrary"),
        name="moe_up",
    )(tile_expert, x_g, w_gu, w_gu)


def _moe_down(a_g, w_down, tile_expert):
    r, f = a_g.shape
    d = w_down.shape[2]
    return pl.pallas_call(
        _moe_down_kernel,
        grid_spec=pltpu.PrefetchScalarGridSpec(
            num_scalar_prefetch=1,
            grid=(r // MOE_TILE,),
            in_specs=[pl.BlockSpec((MOE_TILE, f), lambda i, te: (i, 0)),
                      pl.BlockSpec((None, f, d), lambda i, te: (te[i], 0, 0))],
            out_specs=pl.BlockSpec((MOE_TILE, d), lambda i, te: (i, 0))),
        out_shape=jax.ShapeDtypeStruct((r, d), F32),
        compiler_params=_params("arbitrary"),
        name="moe_down",
    )(tile_expert, a_g, w_down)


def _gather_kernel(idx_ref, *refs, n_src, weighted):
    if weighted:
        route_ref, src_hbm, o_ref, buf, sem = refs
    else:
        src_hbm, o_ref, buf, sem = refs
    tile = o_ref.shape[0]

    def row_copy(s, r, src_row):
        return pltpu.make_async_copy(src_hbm.at[pl.ds(src_row, 1)], buf.at[s, pl.ds(r, 1)], sem)

    def start(r, carry):
        for s in range(n_src):
            row_copy(s, r, idx_ref[s, r]).start()
        return carry

    def wait(r, carry):
        for s in range(n_src):
            row_copy(s, r, 0).wait()
        return carry

    lax.fori_loop(0, tile, start, 0, unroll=8)
    lax.fori_loop(0, tile, wait, 0, unroll=8)
    if weighted:
        route = route_ref[...]
        out = buf[0] * route[:, 2:3]
        for s in range(1, n_src):
            out = out + buf[s] * route[:, 2 + s:3 + s]
    else:
        out = buf[0]
    o_ref[...] = out.astype(o_ref.dtype)


def _gather_rows(idx, src, *, out_dtype, route=None, name):
    n_src, r = idx.shape
    d = src.shape[1]
    in_specs = [pl.BlockSpec((n_src, GATHER_TILE), lambda i: (0, i), memory_space=pltpu.SMEM)]
    args = [idx]
    if route is not None:
        in_specs.append(pl.BlockSpec((GATHER_TILE, LANES), lambda i: (i, 0)))
        args.append(route)
    in_specs.append(pl.BlockSpec(memory_space=pl.ANY))
    args.append(src)
    return pl.pallas_call(
        functools.partial(_gather_kernel, n_src=n_src, weighted=route is not None),
        grid=(r // GATHER_TILE,),
        in_specs=in_specs,
        out_specs=pl.BlockSpec((GATHER_TILE, d), lambda i: (i, 0)),
        out_shape=jax.ShapeDtypeStruct((r, d), out_dtype),
        scratch_shapes=[pltpu.VMEM((n_src, GATHER_TILE, d), src.dtype), pltpu.SemaphoreType.DMA],
        compiler_params=_params("arbitrary"),
        name=name,
    )(*args)


def _route_tables(route, n_rows):
    n = route.shape[0]
    expert = jnp.concatenate([route[:, 0], route[:, 1]]).astype(jnp.int32)
    token = jnp.concatenate([jnp.arange(n, dtype=jnp.int32)] * 2)
    onehot = (expert[:, None] == jnp.arange(N_EXPERTS, dtype=jnp.int32)[None, :]).astype(jnp.int32)
    rank = jnp.sum((jnp.cumsum(onehot, axis=0) - onehot) * onehot, axis=1)
    counts = jnp.sum(onehot, axis=0)
    padded = (counts + MOE_TILE - 1) // MOE_TILE * MOE_TILE
    ends = jnp.cumsum(padded)
    pos = (ends - padded)[expert] + rank
    src = jnp.zeros((n_rows,), jnp.int32).at[pos].set(token)
    tile_start = jnp.arange(n_rows // MOE_TILE, dtype=jnp.int32) * MOE_TILE
    tile_expert = jnp.minimum(jnp.searchsorted(ends, tile_start, side="right"), N_EXPERTS - 1).astype(jnp.int32)
    return src[None, :], tile_expert, pos.reshape(2, n)


def _rope_mm_kernel(x_ref, w_ref, cos_ref, sin_ref, o_ref, *, half, rope_every, scale):
    acc = jnp.dot(x_ref[...], w_ref[...], preferred_element_type=F32)
    cos, sin = cos_ref[...], sin_ref[...]
    for g in range(acc.shape[1] // LANES):
        blk = acc[:, g * LANES:(g + 1) * LANES]
        if g % rope_every == rope_every - 1:
            blk = _rope(blk, cos, sin, half)
        o_ref[:, g * LANES:(g + 1) * LANES] = (blk * scale).astype(o_ref.dtype)


def _rope_matmul(x, w, cos, sin, *, tm, tn, half, rope_every, scale, name):
    m, k = x.shape
    n = w.shape[1]
    tn = min(tn, n)
    tpb = cos.shape[0] // tm
    return pl.pallas_call(
        functools.partial(_rope_mm_kernel, half=half, rope_every=rope_every, scale=scale),
        grid=(m // tm, n // tn),
        in_specs=[pl.BlockSpec((tm, k), lambda i, j: (i, 0)),
                  pl.BlockSpec((k, tn), lambda i, j: (0, j)),
                  pl.BlockSpec((tm, LANES), lambda i, j: (i % tpb, 0)),
                  pl.BlockSpec((tm, LANES), lambda i, j: (i % tpb, 0))],
        out_specs=pl.BlockSpec((tm, tn), lambda i, j: (i, j)),
        out_shape=jax.ShapeDtypeStruct((m, n), BF16),
        compiler_params=_params("arbitrary", "arbitrary"),
        name=name,
    )(x, w, cos, sin)


def _vt_kernel(w_ref, x_ref, o_ref, *, dv):
    res = lax.dot_general(w_ref[...], x_ref[...], (((1,), (1,)), ((), ())),
                          preferred_element_type=F32).astype(o_ref.dtype)
    dvp = dv + ONES_ROWS
    ones = jnp.ones((ONES_ROWS, res.shape[1]), o_ref.dtype)
    for h in range(res.shape[0] // dv):
        o_ref[h * dvp:h * dvp + dv, :] = res[h * dv:(h + 1) * dv]
        o_ref[h * dvp + dv:(h + 1) * dvp, :] = ones


def _vt_matmul(w_t, x, *, chunk, tv, dv):
    d_all, k = w_t.shape
    n = x.shape[0]
    tv = min(tv, d_all)
    tvp = tv // dv * (dv + ONES_ROWS)
    return pl.pallas_call(
        functools.partial(_vt_kernel, dv=dv),
        grid=(n // chunk, d_all // tv),
        in_specs=[pl.BlockSpec((tv, k), lambda c, j: (j, 0)),
                  pl.BlockSpec((chunk, k), lambda c, j: (c, 0))],
        out_specs=pl.BlockSpec((None, tvp, chunk), lambda c, j: (c, j, 0)),
        out_shape=jax.ShapeDtypeStruct((n // chunk, d_all // dv * (dv + ONES_ROWS), chunk), BF16),
        compiler_params=_params("arbitrary", "arbitrary"),
        name="vt",
    )(w_t, x)


def _mla_down_kernel(x_ref, w_ref, qn_ref, kvn_ref, cos_ref, sin_ref, cq_ref, cn_ref, kpe_ref):
    acc = jnp.dot(x_ref[...], w_ref[...], preferred_element_type=F32)
    r0, r1 = MLA_Q_RANK, MLA_Q_RANK + MLA_KV_RANK
    cq_ref[...] = _rms(acc[:, :r0], qn_ref[...]).astype(BF16)
    cn_ref[...] = _rms(acc[:, r0:r1], kvn_ref[...]).astype(BF16)
    kpe_ref[...] = _rope(acc[:, r1:], cos_ref[...], sin_ref[...], MLA_ROPE_DIM // 4).astype(BF16)


def _mla_down(h, w_cat, q_norm, kv_norm, cos, sin, *, tm):
    m, k = h.shape
    n = w_cat.shape[1]
    tpb = cos.shape[0] // tm
    row = lambda width: pl.BlockSpec((tm, width), lambda i: (i, 0))
    tab = pl.BlockSpec((tm, LANES), lambda i: (i % tpb, 0))
    return pl.pallas_call(
        _mla_down_kernel,
        grid=(m // tm,),
        in_specs=[row(k),
                  pl.BlockSpec((k, n), lambda i: (0, 0)),
                  pl.BlockSpec((1, MLA_Q_RANK), lambda i: (0, 0)),
                  pl.BlockSpec((1, MLA_KV_RANK), lambda i: (0, 0)),
                  tab, tab],
        out_specs=[row(MLA_Q_RANK), row(MLA_KV_RANK), row(LANES)],
        out_shape=[jax.ShapeDtypeStruct((m, MLA_Q_RANK), BF16),
                   jax.ShapeDtypeStruct((m, MLA_KV_RANK), BF16),
                   jax.ShapeDtypeStruct((m, LANES), BF16)],
        compiler_params=_params("arbitrary"),
        name="mla_down",
    )(h, w_cat, q_norm, kv_norm, cos, sin)


def _mla_kup_kernel(c_ref, w_ref, kpe_ref, o_ref):
    acc = jnp.dot(c_ref[...], w_ref[...], preferred_element_type=F32).astype(BF16)
    kpe = kpe_ref[...]
    for h in range(MLA_HEADS):
        o_ref[:, h * MLA_QK_PAD:h * MLA_QK_PAD + LANES] = acc[:, h * LANES:(h + 1) * LANES]
        o_ref[:, h * MLA_QK_PAD + LANES:(h + 1) * MLA_QK_PAD] = kpe


def _mla_kup(cn, w_uk, kpe, *, tm):
    m, k = cn.shape
    n_out = MLA_HEADS * MLA_QK_PAD
    return pl.pallas_call(
        _mla_kup_kernel,
        grid=(m // tm,),
        in_specs=[pl.BlockSpec((tm, k), lambda i: (i, 0)),
                  pl.BlockSpec((k, MLA_HEADS * MLA_NOPE_DIM), lambda i: (0, 0)),
                  pl.BlockSpec((tm, LANES), lambda i: (i, 0))],
        out_specs=pl.BlockSpec((tm, n_out), lambda i: (i, 0)),
        out_shape=jax.ShapeDtypeStruct((m, n_out), BF16),
        compiler_params=_params("arbitrary"),
        name="mla_kup",
    )(cn, w_uk, kpe)


def _fold_rows(op, s):
    r = s.shape[0]
    while r > 8 and r % 16 == 0:
        r //= 2
        s = op(s[:r], s[r:])
    return s


def _scores(k, q):
    return lax.dot_general(k, q, (((1,), (1,)), ((), ())), preferred_element_type=F32)


def _attend(q_ref, k_ref, vt_ref, o_ref, acc_ref, finish, *, n_maps, n_chunks, chunk, ctx_len, tq):
    tokens, dqk = q_ref.shape
    dv = vt_ref.shape[1] - ONES_ROWS
    dm = dqk // n_maps
    n_lanes = ATTN_STREAMS * n_maps

    def colmax(s):
        return jnp.max(_fold_rows(jnp.maximum, s), axis=0, keepdims=True)

    def split(acc):
        return acc[dv:dv + 1], acc[:dv]

    def pv(v_t, p):
        return jnp.dot(v_t, p.astype(BF16), preferred_element_type=F32)

    parts = []
    for a in range(n_maps):
        s = _scores(k_ref[0:ctx_len, a * dm:(a + 1) * dm], q_ref[0:ctx_len, a * dm:(a + 1) * dm])
        parts.append(split(pv(vt_ref[0, :, 0:ctx_len], jnp.exp2(s - colmax(s)))))
    o_ref[0:ctx_len, :] = finish(parts)

    def q_body(i, _):
        offs = [pl.multiple_of(ctx_len + (i * ATTN_STREAMS + j) * tq, math.gcd(ctx_len, tq))
                for j in range(ATTN_STREAMS)]

        def lane_scores(c, w):
            j, a = divmod(w, n_maps)
            k = k_ref[pl.ds(pl.multiple_of(c * chunk, chunk), chunk), a * dm:(a + 1) * dm]
            return _scores(k, q_ref[pl.ds(offs[j], tq), a * dm:(a + 1) * dm])

        def write_out():
            for j, off in enumerate(offs):
                o_ref[pl.ds(off, tq), :] = finish([split(acc_ref[j * n_maps + a]) for a in range(n_maps)])

        shifts = []
        for w in range(n_lanes):
            s = lane_scores(0, w)
            m = colmax(s)
            acc_ref[w] = pv(vt_ref[0], jnp.exp2(s - m))
            shifts.append(m)

        def fast_body(c, carry):
            shifts, over = carry
            v_t = vt_ref[c]
            out_shifts, out_over = [], []
            for w in range(n_lanes):
                s = lane_scores(c, w)
                m_old = shifts[w]
                cm = colmax(s)
                m_new = jnp.maximum(m_old, cm)
                acc_ref[w] = (acc_ref[w] + pv(v_t, jnp.exp2(s - m_old))) * jnp.exp2(m_old - m_new)
                out_shifts.append(m_new)
                out_over.append(jnp.maximum(over[w], cm - m_old))
            return tuple(out_shifts), tuple(out_over)

        zero = jnp.zeros((1, tq), F32)
        carry = (tuple(shifts), (zero,) * n_lanes)
        n_pairs = (n_chunks - 1) // 2
        carry = lax.fori_loop(0, n_pairs, lambda t, cr: fast_body(2 * t + 2, fast_body(2 * t + 1, cr)), carry)
        if (n_chunks - 1) % 2:
            carry = fast_body(n_chunks - 1, carry)
        _, over = carry
        write_out()

        worst = over[0]
        for w in range(1, n_lanes):
            worst = jnp.maximum(worst, over[w])

        @pl.when(jnp.max(worst) > OVERSHOOT_LIMIT)
        def _():
            def safe_body(c, ms):
                v_t = vt_ref[c]
                out = []
                for w in range(n_lanes):
                    s = lane_scores(c, w)
                    m_new = jnp.maximum(ms[w], colmax(s))
                    acc_ref[w] = acc_ref[w] * jnp.exp2(ms[w] - m_new) + pv(v_t, jnp.exp2(s - m_new))
                    out.append(m_new)
                return tuple(out)

            acc_ref[...] = jnp.zeros(acc_ref.shape, F32)
            lax.fori_loop(0, n_chunks, safe_body, (jnp.full((1, tq), -jnp.inf, F32),) * n_lanes)
            write_out()

        return 0

    lax.fori_loop(0, (tokens - ctx_len) // (tq * ATTN_STREAMS), q_body, 0)


def _mla_attn_kernel(q_ref, k_ref, vt_ref, o_ref, acc_ref, **kw):
    def finish(parts):
        ((l, acc),) = parts
        return (acc / l).T.astype(o_ref.dtype)

    _attend(q_ref, k_ref, vt_ref, o_ref, acc_ref, finish, n_maps=1, **kw)


def _diff_attn_kernel(lam_ref, sub_ref, q_ref, k_ref, vt_ref, o_ref, acc_ref, *, lambda_init, **kw):
    def finish(parts):
        (l1, a1), (l2, a2) = parts
        lp = lam_ref[...]
        lam = (jnp.exp(jnp.sum(lp[0:1] * lp[1:2], keepdims=True))
               - jnp.exp(jnp.sum(lp[2:3] * lp[3:4], keepdims=True)) + lambda_init)
        o = (a1 / l1 - lam * (a2 / l2)).T
        return (_rms(o, sub_ref[...]) * (1.0 - lambda_init)).astype(o_ref.dtype)

    _attend(q_ref, k_ref, vt_ref, o_ref, acc_ref, finish, n_maps=2, **kw)


def _attention(kernel, q, k, vt, *, n_batch, n_heads, n_maps, dqk, dv, ctx_len, tq, chunk, extra=(), name):
    n = q.shape[0]
    tokens = n // n_batch
    n_chunks = tokens // chunk
    n_lanes = ATTN_STREAMS * n_maps
    dvp = dv + ONES_ROWS
    extra_specs = [pl.BlockSpec(e.shape, lambda b, h: (0, 0)) for e in extra]
    return pl.pallas_call(
        functools.partial(kernel, n_chunks=n_chunks, chunk=chunk, ctx_len=ctx_len, tq=tq),
        grid=(n_batch, n_heads),
        in_specs=extra_specs + [
            pl.BlockSpec((tokens, dqk), lambda b, h: (b, h)),
            pl.BlockSpec((tokens, dqk), lambda b, h: (b, h)),
            pl.BlockSpec((n_chunks, dvp, chunk), lambda b, h: (b, h, 0)),
        ],
        out_specs=pl.BlockSpec((tokens, dv), lambda b, h: (b, h)),
        out_shape=jax.ShapeDtypeStruct((n, n_heads * dv), BF16),
        scratch_shapes=[pltpu.VMEM((n_lanes, dvp, tq), F32)],
        compiler_params=_params("arbitrary", "arbitrary"),
        name=name,
    )(*extra, q, k, vt)


def _rope_tables(ctx_len, seq, rot_dim):
    half = rot_dim // 2
    n_freq = half // 2
    t = jnp.arange(seq, dtype=jnp.int32)
    pos_row = (t // GRID_W).astype(F32)
    pos_col = (t % GRID_W).astype(F32)
    inv = ROPE_BASE ** (-jnp.arange(0, half, 2, dtype=F32) / half)
    lane = jnp.arange(LANES)
    freq = inv[(lane % half) % n_freq]
    pos = jnp.where((lane // half)[None, :] == 0, pos_row[:, None], pos_col[:, None])
    ang = pos * freq[None, :]
    live = (lane < rot_dim)[None, :]
    sign = jnp.where((lane % half) < n_freq, -1.0, 1.0)[None, :]
    cos = jnp.where(live, jnp.cos(ang), 1.0)
    sin = jnp.where(live, jnp.sin(ang) * sign, 0.0)
    cos = jnp.concatenate([jnp.ones((ctx_len, LANES), F32), cos], axis=0)
    sin = jnp.concatenate([jnp.zeros((ctx_len, LANES), F32), sin], axis=0)
    return cos, sin


def kernel(x, c, ctx, c_ctx, ada_w, ada_b, norm_g, mla_w_dq, mla_q_norm, mla_w_uq, mla_w_dkv, mla_kv_norm, mla_w_ukv, mla_w_o, diff_w_qkv, diff_lambda, diff_subln, diff_w_o, ffn_w_gu, ffn_w_down, moe_router, moe_w_gu, moe_w_down):
    n_batch, seq, d = x.shape
    ctx_len = ctx.shape[1]
    depth = ada_w.shape[0]
    tokens = ctx_len + seq
    tm = TOKEN_TILE
    assert tokens % tm == 0 and tokens % ctx_len == 0 and ctx_len % LANES == 0 and ctx_len <= tm
    assert n_batch < MOD_ROWS and seq % (ATTN_Q_TILE * ATTN_STREAMS) == 0
    diff_heads = d // (2 * DIFF_HEAD_DIM)

    xs = jnp.concatenate([ctx, x], axis=1).reshape(n_batch * tokens, d)
    cond = jnp.zeros((MOD_ROWS, d), F32).at[:n_batch].set(c).at[n_batch].set(c_ctx)
    ada_b3 = ada_b.reshape(depth, 1, N_MOD * d)
    stream = functools.partial(_stream, n_batch=n_batch, tile=ctx_len)

    cos_d, sin_d = _rope_tables(ctx_len, seq, DIFF_HEAD_DIM)
    cos_m, sin_m = _rope_tables(ctx_len, seq, MLA_ROPE_DIM)

    mods = _ada(cond, ada_w, ada_b3, 0)
    (h,) = stream(xs, g_norm=norm_g[0, 0:1], mods_norm=mods, shift_idx=0)

    for i in range(depth):
        j = i // 2
        g = norm_g[i]
        if i % 2 == 0:
            w_cat = jnp.concatenate(
                [mla_w_dq[j], mla_w_dkv[j], jnp.zeros((d, LANES - MLA_ROPE_DIM), F32)], axis=1).astype(BF16)
            w_uq = mla_w_uq[j].reshape(MLA_Q_RANK, MLA_HEADS, MLA_NOPE_DIM + MLA_ROPE_DIM)
            w_uq = jnp.pad(w_uq, ((0, 0), (0, 0), (0, MLA_QK_PAD - w_uq.shape[2])))
            w_uq = w_uq.reshape(MLA_Q_RANK, MLA_HEADS * MLA_QK_PAD).astype(BF16)
            w_ukv = mla_w_ukv[j].reshape(MLA_KV_RANK, MLA_HEADS, MLA_NOPE_DIM + MLA_V_DIM)
            w_uk = w_ukv[:, :, :MLA_NOPE_DIM].reshape(MLA_KV_RANK, -1).astype(BF16)
            w_uv_t = w_ukv[:, :, MLA_NOPE_DIM:].reshape(MLA_KV_RANK, -1).T.astype(BF16)

            cq, cn, kpe = _mla_down(h, w_cat, mla_q_norm[j][None], mla_kv_norm[j][None], cos_m, sin_m, tm=tm)
            q = _rope_matmul(cq, w_uq, cos_m, sin_m, tm=tm, tn=2048, half=MLA_ROPE_DIM // 4,
                             rope_every=2, scale=MLA_SCALE * LOG2E, name="mla_q")
            k = _mla_kup(cn, w_uk, kpe, tm=tm)
            vt = _vt_matmul(w_uv_t, cn, chunk=tm, tv=w_uv_t.shape[0], dv=MLA_V_DIM)
            o = _attention(_mla_attn_kernel, q, k, vt, n_batch=n_batch, n_heads=MLA_HEADS, n_maps=1,
                           dqk=MLA_QK_PAD, dv=MLA_V_DIM, ctx_len=ctx_len, tq=ATTN_Q_TILE, chunk=tm,
                           name="mla_attn")
            y = _matmul(o, mla_w_o[j].astype(BF16), tm=tm, tn=1024, name="mla_o")
        else:
            lambda_init = 0.8 - 0.6 * math.exp(-0.3 * i)
            w_q = diff_w_qkv[j][:, :d].astype(BF16)
            w_k = diff_w_qkv[j][:, d:2 * d].astype(BF16)
            w_v_t = diff_w_qkv[j][:, 2 * d:].T.astype(BF16)
            scale = DIFF_HEAD_DIM ** -0.5 * LOG2E
            q = _rope_matmul(h, w_q, cos_d, sin_d, tm=tm, tn=1024, half=DIFF_HEAD_DIM // 4,
                             rope_every=1, scale=scale, name="diff_q")
            k = _rope_matmul(h, w_k, cos_d, sin_d, tm=tm, tn=1024, half=DIFF_HEAD_DIM // 4,
                             rope_every=1, scale=1.0, name="diff_k")
            vt = _vt_matmul(w_v_t, h, chunk=tm, tv=1024, dv=2 * DIFF_HEAD_DIM)
            o = _attention(functools.partial(_diff_attn_kernel, lambda_init=lambda_init), q, k, vt,
                           n_batch=n_batch, n_heads=diff_heads, n_maps=2, dqk=2 * DIFF_HEAD_DIM,
                           dv=2 * DIFF_HEAD_DIM, ctx_len=ctx_len, tq=ATTN_Q_TILE, chunk=tm,
                           extra=(diff_lambda[j], diff_subln[j][None]), name="diff_attn")
            y = _matmul(o, diff_w_o[j].astype(BF16), tm=tm, tn=1024, name="diff_o")

        if i % 2 == 0:
            xs, h = stream(xs, y=y, g_res=g[1:2], mods_res=mods, gate_idx=2,
                           g_norm=g[2:3], mods_norm=mods, shift_idx=3)
            a = _swiglu_dense(h, ffn_w_gu[j].astype(BF16), tm=tm, tn=512)
            f = _matmul(a, ffn_w_down[j].astype(BF16), tm=tm, tn=1024, name="ffn_down")
        else:
            w_r = jnp.pad(moe_router[j], ((0, 0), (0, LANES - N_EXPERTS)))
            xs, h32, route = stream(xs, y=y, g_res=g[1:2], mods_res=mods, gate_idx=2,
                                    g_norm=g[2:3], mods_norm=mods, shift_idx=3, w_router=w_r)
            n_rows = 2 * xs.shape[0] + N_EXPERTS * MOE_TILE
            src, tile_expert, pos = _route_tables(route, n_rows)
            x_g = _gather_rows(src, h32, out_dtype=BF16, name="moe_dispatch")
            a_g = _moe_up(x_g, moe_w_gu[j].astype(BF16), tile_expert, tn=512)
            y_g = _moe_down(a_g, moe_w_down[j].astype(BF16), tile_expert)
            f = _gather_rows(pos, y_g, out_dtype=BF16, route=route, name="moe_combine")

        if i + 1 < depth:
            mods_next = _ada(cond, ada_w, ada_b3, i + 1)
            xs, h = stream(xs, y=f, g_res=g[3:4], mods_res=mods, gate_idx=5,
                           g_norm=norm_g[i + 1, 0:1], mods_norm=mods_next, shift_idx=0)
            mods = mods_next
        else:
            (out,) = stream(xs, y=f, g_res=g[3:4], mods_res=mods, gate_idx=5, latent_only=True)

    return out.reshape(n_batch, seq, d)
```

```python
import functools
import math

import jax
import jax.numpy as jnp
from jax import lax
from jax.experimental import pallas as pl
from jax.experimental.pallas import tpu as pltpu

F32 = jnp.float32
BF16 = jnp.bfloat16

GRID_W = 64
ROPE_BASE = 10000.0
EPS = 1e-6
N_MOD = 6

MLA_HEADS = 32
MLA_Q_RANK = 1024
MLA_KV_RANK = 512
MLA_NOPE_DIM = 128
MLA_ROPE_DIM = 64
MLA_V_DIM = 128
MLA_QK_PAD = 256
MLA_SCALE = (MLA_NOPE_DIM + MLA_ROPE_DIM) ** -0.5

DIFF_HEAD_DIM = 128

N_EXPERTS = 8

LANES = 128
MOD_ROWS = 8
TOKEN_TILE = 768
MOE_TILE = 512
GATHER_TILE = 512
ATTN_Q_TILE = 512
ATTN_STREAMS = 4
ONES_ROWS = 16
OVERSHOOT_LIMIT = 60.0
VMEM_LIMIT = 56 * 1024 * 1024
LOG2E = math.log2(math.e)


def _params(*semantics):
    return pltpu.CompilerParams(dimension_semantics=semantics, vmem_limit_bytes=VMEM_LIMIT)


def _rms(x, g):
    return x * lax.rsqrt(jnp.mean(x * x, axis=-1, keepdims=True) + EPS) * g


def _swap_halves(x, half):
    lane = lax.broadcasted_iota(jnp.int32, x.shape, 1)
    first = (lane % (2 * half)) < half
    return jnp.where(first, pltpu.roll(x, LANES - half, 1), pltpu.roll(x, half, 1))


def _rope(x, cos, sin, half):
    return x * cos + _swap_halves(x, half) * sin


def _ada_kernel(c_ref, w_ref, b_ref, o_ref):
    c = c_ref[...]
    s = (c * jax.nn.sigmoid(c)).astype(BF16)
    o_ref[...] = jnp.dot(s, w_ref[...].astype(BF16), preferred_element_type=F32) + b_ref[...]


def _ada(cond, ada_w, ada_b3, layer):
    d = cond.shape[1]
    n = ada_w.shape[2]
    tn = 512
    return pl.pallas_call(
        _ada_kernel,
        grid=(n // tn,),
        in_specs=[
            pl.BlockSpec((MOD_ROWS, d), lambda j: (0, 0)),
            pl.BlockSpec((None, d, tn), lambda j: (layer, 0, j)),
            pl.BlockSpec((None, 1, tn), lambda j: (layer, 0, j)),
        ],
        out_specs=pl.BlockSpec((MOD_ROWS, tn), lambda j: (0, j)),
        out_shape=jax.ShapeDtypeStruct((MOD_ROWS, n), F32),
        compiler_params=_params("arbitrary"),
        name="ada",
    )(cond, ada_w, ada_b3)


def _stream_kernel(*refs, n_batch, d, resid, norm, router, gate_idx, shift_idx, latent_only):
    refs = list(refs)
    x_ref = refs.pop(0)
    if resid:
        y_ref, gres_ref, mres_ref = refs.pop(0), refs.pop(0), refs.pop(0)
    if norm:
        gnorm_ref, mnorm_ref = refs.pop(0), refs.pop(0)
    if router:
        wr_ref = refs.pop(0)
    if resid:
        xo_ref = refs.pop(0)
    if norm:
        h_ref = refs.pop(0)
    if router:
        comb_ref = refs.pop(0)

    row = pl.program_id(0) if latent_only else jnp.where(pl.program_id(1) == 0, n_batch, pl.program_id(0))

    def mod(ref, k):
        return ref[pl.ds(row, 1), pl.ds(k * d, d)]

    x = x_ref[...]
    if resid:
        x = x + mod(mres_ref, gate_idx) * _rms(y_ref[...].astype(F32), gres_ref[...])
        xo_ref[...] = x
    if norm:
        h = _rms(x, gnorm_ref[...]) * (1.0 + mod(mnorm_ref, shift_idx + 1)) + mod(mnorm_ref, shift_idx)
        h_ref[...] = h.astype(h_ref.dtype)
    if router:
        logits = jnp.dot(h, wr_ref[...], precision=lax.Precision.HIGHEST, preferred_element_type=F32)
        lane = lax.broadcasted_iota(jnp.int32, logits.shape, 1)
        neg = jnp.float32(-jnp.inf)
        lg = jnp.where(lane < N_EXPERTS, logits, neg)
        m1 = jnp.max(lg, axis=1, keepdims=True)
        i1 = jnp.min(jnp.where(lg == m1, lane, LANES), axis=1, keepdims=True)
        lg2 = jnp.where(lane == i1, neg, lg)
        m2 = jnp.max(lg2, axis=1, keepdims=True)
        i2 = jnp.min(jnp.where(lg2 == m2, lane, LANES), axis=1, keepdims=True)
        e2 = jnp.exp(m2 - m1)
        w1 = 1.0 / (1.0 + e2)
        w2 = e2 / (1.0 + e2)
        route = jnp.where(lane == 0, i1.astype(F32), jnp.where(lane == 1, i2.astype(F32),
                          jnp.where(lane == 2, w1, jnp.where(lane == 3, w2, 0.0))))
        comb_ref[...] = route


def _stream(x, *, n_batch, tile, y=None, g_res=None, mods_res=None, gate_idx=0,
            g_norm=None, mods_norm=None, shift_idx=0, w_router=None, latent_only=False):
    n, d = x.shape
    resid, norm, router = y is not None, g_norm is not None, w_router is not None
    tpb = n // n_batch // tile
    if latent_only:
        assert resid and not norm
        return pl.pallas_call(
            functools.partial(_stream_kernel, n_batch=n_batch, d=d, resid=True, norm=False, router=False,
                              gate_idx=gate_idx, shift_idx=shift_idx, latent_only=True),
            grid=(n_batch, tpb - 1),
            in_specs=[pl.BlockSpec((tile, d), lambda b, t: (b * tpb + t + 1, 0)),
                      pl.BlockSpec((tile, d), lambda b, t: (b * tpb + t + 1, 0)),
                      pl.BlockSpec((1, d), lambda b, t: (0, 0)),
                      pl.BlockSpec((MOD_ROWS, N_MOD * d), lambda b, t: (0, 0))],
            out_specs=[pl.BlockSpec((tile, d), lambda b, t: (b * (tpb - 1) + t, 0))],
            out_shape=[jax.ShapeDtypeStruct((n - n_batch * tile, d), F32)],
            compiler_params=_params("arbitrary", "arbitrary"),
            name="stream_out",
        )(x, y, g_res, mods_res)
    row_spec = pl.BlockSpec((tile, d), lambda b, t: (b * tpb + t, 0))
    vec_spec = pl.BlockSpec((1, d), lambda b, t: (0, 0))
    mod_spec = pl.BlockSpec((MOD_ROWS, N_MOD * d), lambda b, t: (0, 0))
    args, in_specs, out_shape, out_specs = [x], [row_spec], [], []
    if resid:
        args += [y, g_res, mods_res]
        in_specs += [row_spec, vec_spec, mod_spec]
        out_shape.append(jax.ShapeDtypeStruct((n, d), F32))
        out_specs.append(row_spec)
    if norm:
        args += [g_norm, mods_norm]
        in_specs += [vec_spec, mod_spec]
        out_shape.append(jax.ShapeDtypeStruct((n, d), F32 if router else BF16))
        out_specs.append(row_spec)
    if router:
        args.append(w_router)
        in_specs.append(pl.BlockSpec((d, LANES), lambda b, t: (0, 0)))
        out_shape.append(jax.ShapeDtypeStruct((n, LANES), F32))
        out_specs.append(pl.BlockSpec((tile, LANES), lambda b, t: (b * tpb + t, 0)))
    kernel = functools.partial(_stream_kernel, n_batch=n_batch, d=d, resid=resid, norm=norm,
                               router=router, gate_idx=gate_idx, shift_idx=shift_idx, latent_only=False)
    return pl.pallas_call(
        kernel,
        grid=(n_batch, tpb),
        in_specs=in_specs,
        out_specs=out_specs,
        out_shape=out_shape,
        input_output_aliases={0: 0} if resid else {},
        compiler_params=_params("arbitrary", "arbitrary"),
        name="stream",
    )(*args)


def _mm_kernel(x_ref, w_ref, o_ref):
    o_ref[...] = jnp.dot(x_ref[...], w_ref[...], preferred_element_type=F32).astype(o_ref.dtype)


def _matmul(x, w, *, tm, tn, out_dtype=BF16, name="mm"):
    m, k = x.shape
    n = w.shape[1]
    tn = min(tn, n)
    return pl.pallas_call(
        _mm_kernel,
        grid=(m // tm, n // tn),
        in_specs=[pl.BlockSpec((tm, k), lambda i, j: (i, 0)),
                  pl.BlockSpec((k, tn), lambda i, j: (0, j))],
        out_specs=pl.BlockSpec((tm, tn), lambda i, j: (i, j)),
        out_shape=jax.ShapeDtypeStruct((m, n), out_dtype),
        compiler_params=_params("arbitrary", "arbitrary"),
        name=name,
    )(x, w)


def _swiglu_kernel(x_ref, wg_ref, wu_ref, o_ref):
    x = x_ref[...]
    g = jnp.dot(x, wg_ref[...], preferred_element_type=F32)
    u = jnp.dot(x, wu_ref[...], preferred_element_type=F32)
    o_ref[...] = (g * jax.nn.sigmoid(g) * u).astype(o_ref.dtype)


def _swiglu_dense(x, w_gu, *, tm, tn):
    m, k = x.shape
    f = w_gu.shape[1] // 2
    nj = f // tn
    return pl.pallas_call(
        _swiglu_kernel,
        grid=(m // tm, nj),
        in_specs=[pl.BlockSpec((tm, k), lambda i, j: (i, 0)),
                  pl.BlockSpec((k, tn), lambda i, j: (0, j)),
                  pl.BlockSpec((k, tn), lambda i, j: (0, nj + j))],
        out_specs=pl.BlockSpec((tm, tn), lambda i, j: (i, j)),
        out_shape=jax.ShapeDtypeStruct((m, f), BF16),
        compiler_params=_params("arbitrary", "arbitrary"),
        name="swiglu",
    )(x, w_gu, w_gu)


def _moe_up_kernel(te_ref, x_ref, wg_ref, wu_ref, o_ref):
    _swiglu_kernel(x_ref, wg_ref, wu_ref, o_ref)


def _moe_down_kernel(te_ref, a_ref, w_ref, o_ref):
    _mm_kernel(a_ref, w_ref, o_ref)


def _moe_up(x_g, w_gu, tile_expert, *, tn):
    r, k = x_g.shape
    f = w_gu.shape[2] // 2
    nj = f // tn
    return pl.pallas_call(
        _moe_up_kernel,
        grid_spec=pltpu.PrefetchScalarGridSpec(
            num_scalar_prefetch=1,
            grid=(r // MOE_TILE, nj),
            in_specs=[pl.BlockSpec((MOE_TILE, k), lambda i, j, te: (i, 0)),
                      pl.BlockSpec((None, k, tn), lambda i, j, te: (te[i], 0, j)),
                      pl.BlockSpec((None, k, tn), lambda i, j, te: (te[i], 0, nj + j))],
            out_specs=pl.BlockSpec((MOE_TILE, tn), lambda i, j, te: (i, j))),
        out_shape=jax.ShapeDtypeStruct((r, f), BF16),
        compiler_params=_params("arbitrary", "arbitrary"),
        name="moe_up",
    )(tile_expert, x_g, w_gu, w_gu)


def _moe_down(a_g, w_down, tile_expert):
    r, f = a_g.shape
    d = w_down.shape[2]
    return pl.pallas_call(
        _moe_down_kernel,
        grid_spec=pltpu.PrefetchScalarGridSpec(
            num_scalar_prefetch=1,
            grid=(r // MOE_TILE,),
            in_specs=[pl.BlockSpec((MOE_TILE, f), lambda i, te: (i, 0)),
                      pl.BlockSpec((None, f, d), lambda i, te: (te[i], 0, 0))],
            out_specs=pl.BlockSpec((MOE_TILE, d), lambda i, te: (i, 0))),
        out_shape=jax.ShapeDtypeStruct((r, d), F32),
        compiler_params=_params("arbitrary"),
        name="moe_down",
    )(tile_expert, a_g, w_down)


def _gather_kernel(idx_ref, *refs, n_src, weighted):
    if weighted:
        route_ref, src_hbm, o_ref, buf, sem = refs
    else:
        src_hbm, o_ref, buf, sem = refs
    tile = o_ref.shape[0]

    def row_copy(s, r, src_row):
        return pltpu.make_async_copy(src_hbm.at[pl.ds(src_row, 1)], buf.at[s, pl.ds(r, 1)], sem)

    def start(i, carry):
        for p in range(2):
            r = 2 * i + p
            for s in range(n_src):
                row_copy(s, r, idx_ref[s, r]).start(priority=p)
        return carry

    def wait(r, carry):
        for s in range(n_src):
            row_copy(s, r, 0).wait()
        return carry

    lax.fori_loop(0, tile // 2, start, 0, unroll=4)
    lax.fori_loop(0, tile, wait, 0, unroll=8)
    if weighted:
        route = route_ref[...]
        out = buf[0] * route[:, 2:3]
        for s in range(1, n_src):
            out = out + buf[s] * route[:, 2 + s:3 + s]
    else:
        out = buf[0]
    o_ref[...] = out.astype(o_ref.dtype)


def _gather_rows(idx, src, *, out_dtype, route=None, name):
    n_src, r = idx.shape
    d = src.shape[1]
    in_specs = [pl.BlockSpec((n_src, GATHER_TILE), lambda i: (0, i), memory_space=pltpu.SMEM)]
    args = [idx]
    if route is not None:
        in_specs.append(pl.BlockSpec((GATHER_TILE, LANES), lambda i: (i, 0)))
        args.append(route)
    in_specs.append(pl.BlockSpec(memory_space=pl.ANY))
    args.append(src)
    return pl.pallas_call(
        functools.partial(_gather_kernel, n_src=n_src, weighted=route is not None),
        grid=(r // GATHER_TILE,),
        in_specs=in_specs,
        out_specs=pl.BlockSpec((GATHER_TILE, d), lambda i: (i, 0)),
        out_shape=jax.ShapeDtypeStruct((r, d), out_dtype),
        scratch_shapes=[pltpu.VMEM((n_src, GATHER_TILE, d), src.dtype), pltpu.SemaphoreType.DMA],
        compiler_params=_params("arbitrary"),
        name=name,
    )(*args)


def _route_tables(route, n_rows):
    n = route.shape[0]
    expert = jnp.concatenate([route[:, 0], route[:, 1]]).astype(jnp.int32)
    token = jnp.concatenate([jnp.arange(n, dtype=jnp.int32)] * 2)
    onehot = (expert[:, None] == jnp.arange(N_EXPERTS, dtype=jnp.int32)[None, :]).astype(jnp.int32)
    rank = jnp.sum((jnp.cumsum(onehot, axis=0) - onehot) * onehot, axis=1)
    counts = jnp.sum(onehot, axis=0)
    padded = (counts + MOE_TILE - 1) // MOE_TILE * MOE_TILE
    ends = jnp.cumsum(padded)
    pos = (ends - padded)[expert] + rank
    src = jnp.zeros((n_rows,), jnp.int32).at[pos].set(token)
    tile_start = jnp.arange(n_rows // MOE_TILE, dtype=jnp.int32) * MOE_TILE
    tile_expert = jnp.minimum(jnp.searchsorted(ends, tile_start, side="right"), N_EXPERTS - 1).astype(jnp.int32)
    return src[None, :], tile_expert, pos.reshape(2, n)


def _rope_mm_kernel(x_ref, w_ref, cos_ref, sin_ref, o_ref, *, half, rope_every, scale):
    acc = jnp.dot(x_ref[...], w_ref[...], preferred_element_type=F32)
    cos, sin = cos_ref[...], sin_ref[...]
    for g in range(acc.shape[1] // LANES):
        blk = acc[:, g * LANES:(g + 1) * LANES]
        if g % rope_every == rope_every - 1:
            blk = _rope(blk, cos, sin, half)
        o_ref[:, g * LANES:(g + 1) * LANES] = (blk * scale).astype(o_ref.dtype)


def _rope_matmul(x, w, cos, sin, *, tm, tn, half, rope_every, scale, name):
    m, k = x.shape
    n = w.shape[1]
    tn = min(tn, n)
    tpb = cos.shape[0] // tm
    return pl.pallas_call(
        functools.partial(_rope_mm_kernel, half=half, rope_every=rope_every, scale=scale),
        grid=(m // tm, n // tn),
        in_specs=[pl.BlockSpec((tm, k), lambda i, j: (i, 0)),
                  pl.BlockSpec((k, tn), lambda i, j: (0, j)),
                  pl.BlockSpec((tm, LANES), lambda i, j: (i % tpb, 0)),
                  pl.BlockSpec((tm, LANES), lambda i, j: (i % tpb, 0))],
        out_specs=pl.BlockSpec((tm, tn), lambda i, j: (i, j)),
        out_shape=jax.ShapeDtypeStruct((m, n), BF16),
        compiler_params=_params("arbitrary", "arbitrary"),
        name=name,
    )(x, w, cos, sin)


def _vt_kernel(w_ref, x_ref, o_ref, *, dv):
    res = lax.dot_general(w_ref[...], x_ref[...], (((1,), (1,)), ((), ())),
                          preferred_element_type=F32).astype(o_ref.dtype)
    dvp = dv + ONES_ROWS
    ones = jnp.ones((ONES_ROWS, res.shape[1]), o_ref.dtype)
    for h in range(res.shape[0] // dv):
        o_ref[h * dvp:h * dvp + dv, :] = res[h * dv:(h + 1) * dv]
        o_ref[h * dvp + dv:(h + 1) * dvp, :] = ones


def _vt_matmul(w_t, x, *, chunk, tv, dv):
    d_all, k = w_t.shape
    n = x.shape[0]
    tv = min(tv, d_all)
    tvp = tv // dv * (dv + ONES_ROWS)
    return pl.pallas_call(
        functools.partial(_vt_kernel, dv=dv),
        grid=(n // chunk, d_all // tv),
        in_specs=[pl.BlockSpec((tv, k), lambda c, j: (j, 0)),
                  pl.BlockSpec((chunk, k), lambda c, j: (c, 0))],
        out_specs=pl.BlockSpec((None, tvp, chunk), lambda c, j: (c, j, 0)),
        out_shape=jax.ShapeDtypeStruct((n // chunk, d_all // dv * (dv + ONES_ROWS), chunk), BF16),
        compiler_params=_params("arbitrary", "arbitrary"),
        name="vt",
    )(w_t, x)


def _mla_down_kernel(x_ref, w_ref, qn_ref, kvn_ref, cos_ref, sin_ref, cq_ref, cn_ref, kpe_ref):
    acc = jnp.dot(x_ref[...], w_ref[...], preferred_element_type=F32)
    r0, r1 = MLA_Q_RANK, MLA_Q_RANK + MLA_KV_RANK
    cq_ref[...] = _rms(acc[:, :r0], qn_ref[...]).astype(BF16)
    cn_ref[...] = _rms(acc[:, r0:r1], kvn_ref[...]).astype(BF16)
    kpe_ref[...] = _rope(acc[:, r1:], cos_ref[...], sin_ref[...], MLA_ROPE_DIM // 4).astype(BF16)


def _mla_down(h, w_cat, q_norm, kv_norm, cos, sin, *, tm):
    m, k = h.shape
    n = w_cat.shape[1]
    tpb = cos.shape[0] // tm
    row = lambda width: pl.BlockSpec((tm, width), lambda i: (i, 0))
    tab = pl.BlockSpec((tm, LANES), lambda i: (i % tpb, 0))
    return pl.pallas_call(
        _mla_down_kernel,
        grid=(m // tm,),
        in_specs=[row(k),
                  pl.BlockSpec((k, n), lambda i: (0, 0)),
                  pl.BlockSpec((1, MLA_Q_RANK), lambda i: (0, 0)),
                  pl.BlockSpec((1, MLA_KV_RANK), lambda i: (0, 0)),
                  tab, tab],
        out_specs=[row(MLA_Q_RANK), row(MLA_KV_RANK), row(LANES)],
        out_shape=[jax.ShapeDtypeStruct((m, MLA_Q_RANK), BF16),
                   jax.ShapeDtypeStruct((m, MLA_KV_RANK), BF16),
                   jax.ShapeDtypeStruct((m, LANES), BF16)],
        compiler_params=_params("arbitrary"),
        name="mla_down",
    )(h, w_cat, q_norm, kv_norm, cos, sin)


def _mla_kup_kernel(c_ref, w_ref, kpe_ref, o_ref):
    acc = jnp.dot(c_ref[...], w_ref[...], preferred_element_type=F32).astype(BF16)
    kpe = kpe_ref[...]
    for h in range(MLA_HEADS):
        o_ref[:, h * MLA_QK_PAD:h * MLA_QK_PAD + LANES] = acc[:, h * LANES:(h + 1) * LANES]
        o_ref[:, h * MLA_QK_PAD + LANES:(h + 1) * MLA_QK_PAD] = kpe


def _mla_kup(cn, w_uk, kpe, *, tm):
    m, k = cn.shape
    n_out = MLA_HEADS * MLA_QK_PAD
    return pl.pallas_call(
        _mla_kup_kernel,
        grid=(m // tm,),
        in_specs=[pl.BlockSpec((tm, k), lambda i: (i, 0)),
                  pl.BlockSpec((k, MLA_HEADS * MLA_NOPE_DIM), lambda i: (0, 0)),
                  pl.BlockSpec((tm, LANES), lambda i: (i, 0))],
        out_specs=pl.BlockSpec((tm, n_out), lambda i: (i, 0)),
        out_shape=jax.ShapeDtypeStruct((m, n_out), BF16),
        compiler_params=_params("arbitrary"),
        name="mla_kup",
    )(cn, w_uk, kpe)


def _fold_rows(op, s):
    r = s.shape[0]
    while r > 8 and r % 16 == 0:
        r //= 2
        s = op(s[:r], s[r:])
    return s


def _scores(k, q):
    return lax.dot_general(k, q, (((1,), (1,)), ((), ())), preferred_element_type=F32)


def _attend(q_ref, k_ref, vt_ref, o_ref, acc_ref, finish, *, n_maps, n_chunks, chunk, ctx_len, tq):
    tokens, dqk = q_ref.shape
    dv = vt_ref.shape[1] - ONES_ROWS
    dm = dqk // n_maps
    n_lanes = ATTN_STREAMS * n_maps

    def colmax(s):
        return jnp.max(_fold_rows(jnp.maximum, s), axis=0, keepdims=True)

    def split(acc):
        return acc[dv:dv + 1], acc[:dv]

    def pv(v_t, p):
        return jnp.dot(v_t, p.astype(BF16), preferred_element_type=F32)

    parts = []
    for a in range(n_maps):
        s = _scores(k_ref[0:ctx_len, a * dm:(a + 1) * dm], q_ref[0:ctx_len, a * dm:(a + 1) * dm])
        parts.append(split(pv(vt_ref[0, :, 0:ctx_len], jnp.exp2(s - colmax(s)))))
    o_ref[0:ctx_len, :] = finish(parts)

    def q_body(i, _):
        offs = [pl.multiple_of(ctx_len + (i * ATTN_STREAMS + j) * tq, math.gcd(ctx_len, tq))
                for j in range(ATTN_STREAMS)]

        def lane_scores(c, w):
            j, a = divmod(w, n_maps)
            k = k_ref[pl.ds(pl.multiple_of(c * chunk, chunk), chunk), a * dm:(a + 1) * dm]
            return _scores(k, q_ref[pl.ds(offs[j], tq), a * dm:(a + 1) * dm])

        def write_out():
            for j, off in enumerate(offs):
                o_ref[pl.ds(off, tq), :] = finish([split(acc_ref[j * n_maps + a]) for a in range(n_maps)])

        shifts = []
        for w in range(n_lanes):
            s = lane_scores(0, w)
            m = colmax(s)
            acc_ref[w] = pv(vt_ref[0], jnp.exp2(s - m))
            shifts.append(m)

        def fast_body(c, carry):
            shifts, over = carry
            v_t = vt_ref[c]
            out_shifts, out_over = [], []
            for w in range(n_lanes):
                s = lane_scores(c, w)
                m_old = shifts[w]
                cm = colmax(s)
                m_new = jnp.maximum(m_old, cm)
                acc_ref[w] = (acc_ref[w] + pv(v_t, jnp.exp2(s - m_old))) * jnp.exp2(m_old - m_new)
                out_shifts.append(m_new)
                out_over.append(jnp.maximum(over[w], cm - m_old))
            return tuple(out_shifts), tuple(out_over)

        zero = jnp.zeros((1, tq), F32)
        carry = (tuple(shifts), (zero,) * n_lanes)
        n_pairs = (n_chunks - 1) // 2
        carry = lax.fori_loop(0, n_pairs, lambda t, cr: fast_body(2 * t + 2, fast_body(2 * t + 1, cr)), carry)
        if (n_chunks - 1) % 2:
            carry = fast_body(n_chunks - 1, carry)
        _, over = carry
        write_out()

        worst = over[0]
        for w in range(1, n_lanes):
            worst = jnp.maximum(worst, over[w])

        @pl.when(jnp.max(worst) > OVERSHOOT_LIMIT)
        def _():
            def safe_body(c, ms):
                v_t = vt_ref[c]
                out = []
                for w in range(n_lanes):
                    s = lane_scores(c, w)
                    m_new = jnp.maximum(ms[w], colmax(s))
                    acc_ref[w] = acc_ref[w] * jnp.exp2(ms[w] - m_new) + pv(v_t, jnp.exp2(s - m_new))
                    out.append(m_new)
                return tuple(out)

            acc_ref[...] = jnp.zeros(acc_ref.shape, F32)
            lax.fori_loop(0, n_chunks, safe_body, (jnp.full((1, tq), -jnp.inf, F32),) * n_lanes)
            write_out()

        return 0

    lax.fori_loop(0, (tokens - ctx_len) // (tq * ATTN_STREAMS), q_body, 0)


def _mla_attn_kernel(q_ref, k_ref, vt_ref, o_ref, acc_ref, **kw):
    def finish(parts):
        ((l, acc),) = parts
        return (acc / l).T.astype(o_ref.dtype)

    _attend(q_ref, k_ref, vt_ref, o_ref, acc_ref, finish, n_maps=1, **kw)


def _diff_attn_kernel(lam_ref, sub_ref, q_ref, k_ref, vt_ref, o_ref, acc_ref, *, lambda_init, **kw):
    def finish(parts):
        (l1, a1), (l2, a2) = parts
        lp = lam_ref[...]
        lam = (jnp.exp(jnp.sum(lp[0:1] * lp[1:2], keepdims=True))
               - jnp.exp(jnp.sum(lp[2:3] * lp[3:4], keepdims=True)) + lambda_init)
        o = (a1 / l1 - lam * (a2 / l2)).T
        return (_rms(o, sub_ref[...]) * (1.0 - lambda_init)).astype(o_ref.dtype)

    _attend(q_ref, k_ref, vt_ref, o_ref, acc_ref, finish, n_maps=2, **kw)


def _attention(kernel, q, k, vt, *, n_batch, n_heads, n_maps, dqk, dv, ctx_len, tq, chunk, extra=(), name):
    n = q.shape[0]
    tokens = n // n_batch
    n_chunks = tokens // chunk
    n_lanes = ATTN_STREAMS * n_maps
    dvp = dv + ONES_ROWS
    extra_specs = [pl.BlockSpec(e.shape, lambda b, h: (0, 0)) for e in extra]
    return pl.pallas_call(
        functools.partial(kernel, n_chunks=n_chunks, chunk=chunk, ctx_len=ctx_len, tq=tq),
        grid=(n_batch, n_heads),
        in_specs=extra_specs + [
            pl.BlockSpec((tokens, dqk), lambda b, h: (b, h)),
            pl.BlockSpec((tokens, dqk), lambda b, h: (b, h)),
            pl.BlockSpec((n_chunks, dvp, chunk), lambda b, h: (b, h, 0)),
        ],
        out_specs=pl.BlockSpec((tokens, dv), lambda b, h: (b, h)),
        out_shape=jax.ShapeDtypeStruct((n, n_heads * dv), BF16),
        scratch_shapes=[pltpu.VMEM((n_lanes, dvp, tq), F32)],
        compiler_params=_params("arbitrary", "arbitrary"),
        name=name,
    )(*extra, q, k, vt)


def _rope_tables(ctx_len, seq, rot_dim):
    half = rot_dim // 2
    n_freq = half // 2
    t = jnp.arange(seq, dtype=jnp.int32)
    pos_row = (t // GRID_W).astype(F32)
    pos_col = (t % GRID_W).astype(F32)
    inv = ROPE_BASE ** (-jnp.arange(0, half, 2, dtype=F32) / half)
    lane = jnp.arange(LANES)
    freq = inv[(lane % half) % n_freq]
    pos = jnp.where((lane // half)[None, :] == 0, pos_row[:, None], pos_col[:, None])
    ang = pos * freq[None, :]
    live = (lane < rot_dim)[None, :]
    sign = jnp.where((lane % half) < n_freq, -1.0, 1.0)[None, :]
    cos = jnp.where(live, jnp.cos(ang), 1.0)
    sin = jnp.where(live, jnp.sin(ang) * sign, 0.0)
    cos = jnp.concatenate([jnp.ones((ctx_len, LANES), F32), cos], axis=0)
    sin = jnp.concatenate([jnp.zeros((ctx_len, LANES), F32), sin], axis=0)
    return cos, sin


def kernel(x, c, ctx, c_ctx, ada_w, ada_b, norm_g, mla_w_dq, mla_q_norm, mla_w_uq, mla_w_dkv, mla_kv_norm, mla_w_ukv, mla_w_o, diff_w_qkv, diff_lambda, diff_subln, diff_w_o, ffn_w_gu, ffn_w_down, moe_router, moe_w_gu, moe_w_down):
    n_batch, seq, d = x.shape
    ctx_len = ctx.shape[1]
    depth = ada_w.shape[0]
    tokens = ctx_len + seq
    tm = TOKEN_TILE
    assert tokens % tm == 0 and tokens % ctx_len == 0 and ctx_len % LANES == 0 and ctx_len <= tm
    assert n_batch < MOD_ROWS and seq % (ATTN_Q_TILE * ATTN_STREAMS) == 0
    diff_heads = d // (2 * DIFF_HEAD_DIM)

    xs = jnp.concatenate([ctx, x], axis=1).reshape(n_batch * tokens, d)
    cond = jnp.zeros((MOD_ROWS, d), F32).at[:n_batch].set(c).at[n_batch].set(c_ctx)
    ada_b3 = ada_b.reshape(depth, 1, N_MOD * d)
    stream = functools.partial(_stream, n_batch=n_batch, tile=ctx_len)

    cos_d, sin_d = _rope_tables(ctx_len, seq, DIFF_HEAD_DIM)
    cos_m, sin_m = _rope_tables(ctx_len, seq, MLA_ROPE_DIM)

    mods = _ada(cond, ada_w, ada_b3, 0)
    (h,) = stream(xs, g_norm=norm_g[0, 0:1], mods_norm=mods, shift_idx=0)

    for i in range(depth):
        j = i // 2
        g = norm_g[i]
        if i % 2 == 0:
            w_cat = jnp.concatenate(
                [mla_w_dq[j], mla_w_dkv[j], jnp.zeros((d, LANES - MLA_ROPE_DIM), F32)], axis=1).astype(BF16)
            w_uq = mla_w_uq[j].reshape(MLA_Q_RANK, MLA_HEADS, MLA_NOPE_DIM + MLA_ROPE_DIM)
            w_uq = jnp.pad(w_uq, ((0, 0), (0, 0), (0, MLA_QK_PAD - w_uq.shape[2])))
            w_uq = w_uq.reshape(MLA_Q_RANK, MLA_HEADS * MLA_QK_PAD).astype(BF16)
            w_ukv = mla_w_ukv[j].reshape(MLA_KV_RANK, MLA_HEADS, MLA_NOPE_DIM + MLA_V_DIM)
            w_uk = w_ukv[:, :, :MLA_NOPE_DIM].reshape(MLA_KV_RANK, -1).astype(BF16)
            w_uv_t = w_ukv[:, :, MLA_NOPE_DIM:].reshape(MLA_KV_RANK, -1).T.astype(BF16)

            cq, cn, kpe = _mla_down(h, w_cat, mla_q_norm[j][None], mla_kv_norm[j][None], cos_m, sin_m, tm=tm)
            q = _rope_matmul(cq, w_uq, cos_m, sin_m, tm=tm, tn=2048, half=MLA_ROPE_DIM // 4,
                             rope_every=2, scale=MLA_SCALE * LOG2E, name="mla_q")
            k = _mla_kup(cn, w_uk, kpe, tm=tm)
            vt = _vt_matmul(w_uv_t, cn, chunk=tm, tv=w_uv_t.shape[0], dv=MLA_V_DIM)
            o = _attention(_mla_attn_kernel, q, k, vt, n_batch=n_batch, n_heads=MLA_HEADS, n_maps=1,
                           dqk=MLA_QK_PAD, dv=MLA_V_DIM, ctx_len=ctx_len, tq=ATTN_Q_TILE, chunk=tm,
                           name="mla_attn")
            y = _matmul(o, mla_w_o[j].astype(BF16), tm=tm, tn=1024, name="mla_o")
        else:
            lambda_init = 0.8 - 0.6 * math.exp(-0.3 * i)
            w_q = diff_w_qkv[j][:, :d].astype(BF16)
            w_k = diff_w_qkv[j][:, d:2 * d].astype(BF16)
            w_v_t = diff_w_qkv[j][:, 2 * d:].T.astype(BF16)
            scale = DIFF_HEAD_DIM ** -0.5 * LOG2E
            q = _rope_matmul(h, w_q, cos_d, sin_d, tm=tm, tn=1024, half=DIFF_HEAD_DIM // 4,
                             rope_every=1, scale=scale, name="diff_q")
            k = _rope_matmul(h, w_k, cos_d, sin_d, tm=tm, tn=1024, half=DIFF_HEAD_DIM // 4,
                             rope_every=1, scale=1.0, name="diff_k")
            vt = _vt_matmul(w_v_t, h, chunk=tm, tv=1024, dv=2 * DIFF_HEAD_DIM)
            o = _attention(functools.partial(_diff_attn_kernel, lambda_init=lambda_init), q, k, vt,
                           n_batch=n_batch, n_heads=diff_heads, n_maps=2, dqk=2 * DIFF_HEAD_DIM,
                           dv=2 * DIFF_HEAD_DIM, ctx_len=ctx_len, tq=ATTN_Q_TILE, chunk=tm,
                           extra=(diff_lambda[j], diff_subln[j][None]), name="diff_attn")
            y = _matmul(o, diff_w_o[j].astype(BF16), tm=tm, tn=1024, name="diff_o")

        if i % 2 == 0:
            xs, h = stream(xs, y=y, g_res=g[1:2], mods_res=mods, gate_idx=2,
                           g_norm=g[2:3], mods_norm=mods, shift_idx=3)
            a = _swiglu_dense(h, ffn_w_gu[j].astype(BF16), tm=tm, tn=512)
            f = _matmul(a, ffn_w_down[j].astype(BF16), tm=tm, tn=1024, name="ffn_down")
        else:
            w_r = jnp.pad(moe_router[j], ((0, 0), (0, LANES - N_EXPERTS)))
            xs, h32, route = stream(xs, y=y, g_res=g[1:2], mods_res=mods, gate_idx=2,
                                    g_norm=g[2:3], mods_norm=mods, shift_idx=3, w_router=w_r)
            n_rows = 2 * xs.shape[0] + N_EXPERTS * MOE_TILE
            src, tile_expert, pos = _route_tables(route, n_rows)
            x_g = _gather_rows(src, h32, out_dtype=BF16, name="moe_dispatch")
            a_g = _moe_up(x_g, moe_w_gu[j].astype(BF16), tile_expert, tn=512)
            y_g = _moe_down(a_g, moe_w_down[j].astype(BF16), tile_expert)
            f = _gather_rows(pos, y_g, out_dtype=BF16, route=route, name="moe_combine")

        if i + 1 < depth:
            mods_next = _ada(cond, ada_w, ada_b3, i + 1)
            xs, h = stream(xs, y=f, g_res=g[3:4], mods_res=mods, gate_idx=5,
                           g_norm=norm_g[i + 1, 0:1], mods_norm=mods_next, shift_idx=0)
            mods = mods_next
        else:
            (out,) = stream(xs, y=f, g_res=g[3:4], mods_res=mods, gate_idx=5, latent_only=True)

    return out.reshape(n_batch, seq, d)
```

```python
import functools
import math

import jax
import jax.numpy as jnp
from jax import lax
from jax.experimental import pallas as pl
from jax.experimental.pallas import tpu as pltpu

F32 = jnp.float32
BF16 = jnp.bfloat16

GRID_W = 64
ROPE_BASE = 10000.0
EPS = 1e-6
N_MOD = 6

MLA_HEADS = 32
MLA_Q_RANK = 1024
MLA_KV_RANK = 512
MLA_NOPE_DIM = 128
MLA_ROPE_DIM = 64
MLA_V_DIM = 128
MLA_QK_PAD = 256
MLA_SCALE = (MLA_NOPE_DIM + MLA_ROPE_DIM) ** -0.5

DIFF_HEAD_DIM = 128

N_EXPERTS = 8

LANES = 128
MOD_ROWS = 8
TOKEN_TILE = 768
MOE_TILE = 512
GATHER_TILE = 512
ATTN_Q_TILE = 512
ATTN_STREAMS = 4
ONES_ROWS = 16
OVERSHOOT_LIMIT = 60.0
VMEM_LIMIT = 56 * 1024 * 1024
LOG2E = math.log2(math.e)


def _params(*semantics):
    return pltpu.CompilerParams(dimension_semantics=semantics, vmem_limit_bytes=VMEM_LIMIT)


def _rms(x, g):
    return x * lax.rsqrt(jnp.mean(x * x, axis=-1, keepdims=True) + EPS) * g


def _swap_halves(x, half):
    lane = lax.broadcasted_iota(jnp.int32, x.shape, 1)
    first = (lane % (2 * half)) < half
    return jnp.where(first, pltpu.roll(x, LANES - half, 1), pltpu.roll(x, half, 1))


def _rope(x, cos, sin, half):
    return x * cos + _swap_halves(x, half) * sin


def _ada_kernel(c_ref, w_ref, b_ref, o_ref):
    c = c_ref[...]
    s = (c * jax.nn.sigmoid(c)).astype(BF16)
    o_ref[...] = jnp.dot(s, w_ref[...].astype(BF16), preferred_element_type=F32) + b_ref[...]


def _ada(cond, ada_w, ada_b3, layer):
    d = cond.shape[1]
    n = ada_w.shape[2]
    tn = 512
    return pl.pallas_call(
        _ada_kernel,
        grid=(n // tn,),
        in_specs=[
            pl.BlockSpec((MOD_ROWS, d), lambda j: (0, 0)),
            pl.BlockSpec((None, d, tn), lambda j: (layer, 0, j)),
            pl.BlockSpec((None, 1, tn), lambda j: (layer, 0, j)),
        ],
        out_specs=pl.BlockSpec((MOD_ROWS, tn), lambda j: (0, j)),
        out_shape=jax.ShapeDtypeStruct((MOD_ROWS, n), F32),
        compiler_params=_params("arbitrary"),
        name="ada",
    )(cond, ada_w, ada_b3)


def _stream_kernel(*refs, n_batch, d, resid, norm, router, gate_idx, shift_idx, latent_only):
    refs = list(refs)
    x_ref = refs.pop(0)
    if resid:
        y_ref, gres_ref, mres_ref = refs.pop(0), refs.pop(0), refs.pop(0)
    if norm:
        gnorm_ref, mnorm_ref = refs.pop(0), refs.pop(0)
    if router:
        wr_ref = refs.pop(0)
    if resid:
        xo_ref = refs.pop(0)
    if norm:
        h_ref = refs.pop(0)
    if router:
        comb_ref = refs.pop(0)

    row = pl.program_id(0) if latent_only else jnp.where(pl.program_id(1) == 0, n_batch, pl.program_id(0))

    def mod(ref, k):
        return ref[pl.ds(row, 1), pl.ds(k * d, d)]

    x = x_ref[...]
    if resid:
        x = x + mod(mres_ref, gate_idx) * _rms(y_ref[...].astype(F32), gres_ref[...])
        xo_ref[...] = x
    if norm:
        h = _rms(x, gnorm_ref[...]) * (1.0 + mod(mnorm_ref, shift_idx + 1)) + mod(mnorm_ref, shift_idx)
        h_ref[...] = h.astype(h_ref.dtype)
    if router:
        logits = jnp.dot(h, wr_ref[...], precision=lax.Precision.HIGHEST, preferred_element_type=F32)
        lane = lax.broadcasted_iota(jnp.int32, logits.shape, 1)
        neg = jnp.float32(-jnp.inf)
        lg = jnp.where(lane < N_EXPERTS, logits, neg)
        m1 = jnp.max(lg, axis=1, keepdims=True)
        i1 = jnp.min(jnp.where(lg == m1, lane, LANES), axis=1, keepdims=True)
        lg2 = jnp.where(lane == i1, neg, lg)
        m2 = jnp.max(lg2, axis=1, keepdims=True)
        i2 = jnp.min(jnp.where(lg2 == m2, lane, LANES), axis=1, keepdims=True)
        e2 = jnp.exp(m2 - m1)
        w1 = 1.0 / (1.0 + e2)
        w2 = e2 / (1.0 + e2)
        route = jnp.where(lane == 0, i1.astype(F32), jnp.where(lane == 1, i2.astype(F32),
                          jnp.where(lane == 2, w1, jnp.where(lane == 3, w2, 0.0))))
        comb_ref[...] = route


def _stream(x, *, n_batch, tile, y=None, g_res=None, mods_res=None, gate_idx=0,
            g_norm=None, mods_norm=None, shift_idx=0, w_router=None, latent_only=False):
    n, d = x.shape
    resid, norm, router = y is not None, g_norm is not None, w_router is not None
    tpb = n // n_batch // tile
    if latent_only:
        assert resid and not norm
        return pl.pallas_call(
            functools.partial(_stream_kernel, n_batch=n_batch, d=d, resid=True, norm=False, router=False,
                              gate_idx=gate_idx, shift_idx=shift_idx, latent_only=True),
            grid=(n_batch, tpb - 1),
            in_specs=[pl.BlockSpec((tile, d), lambda b, t: (b * tpb + t + 1, 0)),
                      pl.BlockSpec((tile, d), lambda b, t: (b * tpb + t + 1, 0)),
                      pl.BlockSpec((1, d), lambda b, t: (0, 0)),
                      pl.BlockSpec((MOD_ROWS, N_MOD * d), lambda b, t: (0, 0))],
            out_specs=[pl.BlockSpec((tile, d), lambda b, t: (b * (tpb - 1) + t, 0))],
            out_shape=[jax.ShapeDtypeStruct((n - n_batch * tile, d), F32)],
            compiler_params=_params("arbitrary", "arbitrary"),
            name="stream_out",
        )(x, y, g_res, mods_res)
    row_spec = pl.BlockSpec((tile, d), lambda b, t: (b * tpb + t, 0))
    vec_spec = pl.BlockSpec((1, d), lambda b, t: (0, 0))
    mod_spec = pl.BlockSpec((MOD_ROWS, N_MOD * d), lambda b, t: (0, 0))
    args, in_specs, out_shape, out_specs = [x], [row_spec], [], []
    if resid:
        args += [y, g_res, mods_res]
        in_specs += [row_spec, vec_spec, mod_spec]
        out_shape.append(jax.ShapeDtypeStruct((n, d), F32))
        out_specs.append(row_spec)
    if norm:
        args += [g_norm, mods_norm]
        in_specs += [vec_spec, mod_spec]
        out_shape.append(jax.ShapeDtypeStruct((n, d), F32 if router else BF16))
        out_specs.append(row_spec)
    if router:
        args.append(w_router)
        in_specs.append(pl.BlockSpec((d, LANES), lambda b, t: (0, 0)))
        out_shape.append(jax.ShapeDtypeStruct((n, LANES), F32))
        out_specs.append(pl.BlockSpec((tile, LANES), lambda b, t: (b * tpb + t, 0)))
    kernel = functools.partial(_stream_kernel, n_batch=n_batch, d=d, resid=resid, norm=norm,
                               router=router, gate_idx=gate_idx, shift_idx=shift_idx, latent_only=False)
    return pl.pallas_call(
        kernel,
        grid=(n_batch, tpb),
        in_specs=in_specs,
        out_specs=out_specs,
        out_shape=out_shape,
        input_output_aliases={0: 0} if resid else {},
        compiler_params=_params("arbitrary", "arbitrary"),
        name="stream",
    )(*args)


def _mm_kernel(x_ref, w_ref, o_ref):
    o_ref[...] = jnp.dot(x_ref[...], w_ref[...], preferred_element_type=F32).astype(o_ref.dtype)


def _matmul(x, w, *, tm, tn, out_dtype=BF16, name="mm"):
    m, k = x.shape
    n = w.shape[1]
    tn = min(tn, n)
    return pl.pallas_call(
        _mm_kernel,
        grid=(m // tm, n // tn),
        in_specs=[pl.BlockSpec((tm, k), lambda i, j: (i, 0)),
                  pl.BlockSpec((k, tn), lambda i, j: (0, j))],
        out_specs=pl.BlockSpec((tm, tn), lambda i, j: (i, j)),
        out_shape=jax.ShapeDtypeStruct((m, n), out_dtype),
        compiler_params=_params("arbitrary", "arbitrary"),
        name=name,
    )(x, w)


def _swiglu_kernel(x_ref, wg_ref, wu_ref, o_ref):
    x = x_ref[...]
    g = jnp.dot(x, wg_ref[...], preferred_element_type=F32)
    u = jnp.dot(x, wu_ref[...], preferred_element_type=F32)
    o_ref[...] = (g * jax.nn.sigmoid(g) * u).astype(o_ref.dtype)


def _swiglu_dense(x, w_gu, *, tm, tn):
    m, k = x.shape
    f = w_gu.shape[1] // 2
    nj = f // tn
    return pl.pallas_call(
        _swiglu_kernel,
        grid=(m // tm, nj),
        in_specs=[pl.BlockSpec((tm, k), lambda i, j: (i, 0)),
                  pl.BlockSpec((k, tn), lambda i, j: (0, j)),
                  pl.BlockSpec((k, tn), lambda i, j: (0, nj + j))],
        out_specs=pl.BlockSpec((tm, tn), lambda i, j: (i, j)),
        out_shape=jax.ShapeDtypeStruct((m, f), BF16),
        compiler_params=_params("arbitrary", "arbitrary"),
        name="swiglu",
    )(x, w_gu, w_gu)


def _moe_up_kernel(te_ref, x_ref, wg_ref, wu_ref, o_ref):
    _swiglu_kernel(x_ref, wg_ref, wu_ref, o_ref)


def _moe_down_kernel(te_ref, a_ref, w_ref, o_ref):
    _mm_kernel(a_ref, w_ref, o_ref)


def _moe_up(x_g, w_gu, tile_expert, *, tn):
    r, k = x_g.shape
    f = w_gu.shape[2] // 2
    nj = f // tn
    return pl.pallas_call(
        _moe_up_kernel,
        grid_spec=pltpu.PrefetchScalarGridSpec(
            num_scalar_prefetch=1,
            grid=(r // MOE_TILE, nj),
            in_specs=[pl.BlockSpec((MOE_TILE, k), lambda i, j, te: (i, 0)),
                      pl.BlockSpec((None, k, tn), lambda i, j, te: (te[i], 0, j)),
                      pl.BlockSpec((None, k, tn), lambda i, j, te: (te[i], 0, nj + j))],
            out_specs=pl.BlockSpec((MOE_TILE, tn), lambda i, j, te: (i, j))),
        out_shape=jax.ShapeDtypeStruct((r, f), BF16),
        compiler_params=_params("arbitrary", "arbitrary"),
        name="moe_up",
    )(tile_expert, x_g, w_gu, w_gu)


def _moe_down(a_g, w_down, tile_expert):
    r, f = a_g.shape
    d = w_down.shape[2]
    return pl.pallas_call(
        _moe_down_kernel,
        grid_spec=pltpu.PrefetchScalarGridSpec(
            num_scalar_prefetch=1,
            grid=(r // MOE_TILE,),
            in_specs=[pl.BlockSpec((MOE_TILE, f), lambda i, te: (i, 0)),
                      pl.BlockSpec((None, f, d), lambda i, te: (te[i], 0, 0))],
            out_specs=pl.BlockSpec((MOE_TILE, d), lambda i, te: (i, 0))),
        out_shape=jax.ShapeDtypeStruct((r, d), F32),
        compiler_params=_params("arbitrary"),
        name="moe_down",
    )(tile_expert, a_g, w_down)


def _gather_kernel(idx_ref, *refs, n_src, weighted):
    if weighted:
        route_ref, src_hbm, o_ref, buf, sem = refs
    else:
        src_hbm, o_ref, buf, sem = refs
    tile = o_ref.shape[0]

    def row_copy(s, r, src_row):
        return pltpu.make_async_copy(src_hbm.at[pl.ds(src_row, 1)], buf.at[s, pl.ds(r, 1)], sem)

    def start(i, carry):
        for p in range(2):
            r = 2 * i + p
            for s in range(n_src):
                row_copy(s, r, idx_ref[s, r]).start(priority=p)
        return carry

    def wait(r, carry):
        for s in range(n_src):
            row_copy(s, r, 0).wait()
        return carry

    lax.fori_loop(0, tile // 2, start, 0, unroll=4)
    lax.fori_loop(0, tile, wait, 0, unroll=8)
    if weighted:
        route = route_ref[...]
        out = buf[0] * route[:, 2:3]
        for s in range(1, n_src):
            out = out + buf[s] * route[:, 2 + s:3 + s]
    else:
        out = buf[0]
    o_ref[...] = out.astype(o_ref.dtype)


def _gather_rows(idx, src, *, out_dtype, route=None, name):
    n_src, r = idx.shape
    d = src.shape[1]
    in_specs = [pl.BlockSpec((n_src, GATHER_TILE), lambda i: (0, i), memory_space=pltpu.SMEM)]
    args = [idx]
    if route is not None:
        in_specs.append(pl.BlockSpec((GATHER_TILE, LANES), lambda i: (i, 0)))
        args.append(route)
    in_specs.append(pl.BlockSpec(memory_space=pl.ANY))
    args.append(src)
    return pl.pallas_call(
        functools.partial(_gather_kernel, n_src=n_src, weighted=route is not None),
        grid=(r // GATHER_TILE,),
        in_specs=in_specs,
        out_specs=pl.BlockSpec((GATHER_TILE, d), lambda i: (i, 0)),
        out_shape=jax.ShapeDtypeStruct((r, d), out_dtype),
        scratch_shapes=[pltpu.VMEM((n_src, GATHER_TILE, d), src.dtype), pltpu.SemaphoreType.DMA],
        compiler_params=_params("arbitrary"),
        name=name,
    )(*args)


def _route_tables(route, n_rows):
    n = route.shape[0]
    expert = jnp.concatenate([route[:, 0], route[:, 1]]).astype(jnp.int32)
    token = jnp.concatenate([jnp.arange(n, dtype=jnp.int32)] * 2)
    onehot = (expert[:, None] == jnp.arange(N_EXPERTS, dtype=jnp.int32)[None, :]).astype(jnp.int32)
    rank = jnp.sum((jnp.cumsum(onehot, axis=0) - onehot) * onehot, axis=1)
    counts = jnp.sum(onehot, axis=0)
    padded = (counts + MOE_TILE - 1) // MOE_TILE * MOE_TILE
    ends = jnp.cumsum(padded)
    pos = (ends - padded)[expert] + rank
    src = jnp.zeros((n_rows,), jnp.int32).at[pos].set(token)
    tile_start = jnp.arange(n_rows // MOE_TILE, dtype=jnp.int32) * MOE_TILE
    tile_expert = jnp.minimum(jnp.searchsorted(ends, tile_start, side="right"), N_EXPERTS - 1).astype(jnp.int32)
    return src[None, :], tile_expert, pos.reshape(2, n)


def _rope_mm_kernel(x_ref, w_ref, cos_ref, sin_ref, o_ref, *, half, rope_every, scale, scaled_tiles):
    acc = jnp.dot(x_ref[...], w_ref[...], preferred_element_type=F32)
    cos, sin = cos_ref[...], sin_ref[...]
    if scaled_tiles is not None:
        scale = jnp.where(pl.program_id(1) < scaled_tiles, scale, 1.0)
    for g in range(acc.shape[1] // LANES):
        blk = acc[:, g * LANES:(g + 1) * LANES]
        if g % rope_every == rope_every - 1:
            blk = _rope(blk, cos, sin, half)
        o_ref[:, g * LANES:(g + 1) * LANES] = (blk * scale).astype(o_ref.dtype)


def _rope_matmul(x, w, cos, sin, *, tm, tn, half, rope_every, scale, name, scaled_cols=None):
    m, k = x.shape
    n = w.shape[1]
    tn = min(tn, n)
    tpb = cos.shape[0] // tm
    if scaled_cols is not None:
        tn = min(tn, scaled_cols)
        assert scaled_cols % tn == 0
    scaled_tiles = None if scaled_cols is None else scaled_cols // tn
    return pl.pallas_call(
        functools.partial(_rope_mm_kernel, half=half, rope_every=rope_every, scale=scale,
                          scaled_tiles=scaled_tiles),
        grid=(m // tm, n // tn),
        in_specs=[pl.BlockSpec((tm, k), lambda i, j: (i, 0)),
                  pl.BlockSpec((k, tn), lambda i, j: (0, j)),
                  pl.BlockSpec((tm, LANES), lambda i, j: (i % tpb, 0)),
                  pl.BlockSpec((tm, LANES), lambda i, j: (i % tpb, 0))],
        out_specs=pl.BlockSpec((tm, tn), lambda i, j: (i, j)),
        out_shape=jax.ShapeDtypeStruct((m, n), BF16),
        compiler_params=_params("arbitrary", "arbitrary"),
        name=name,
    )(x, w, cos, sin)


def _vt_kernel(w_ref, x_ref, o_ref, *, dv):
    res = lax.dot_general(w_ref[...], x_ref[...], (((1,), (1,)), ((), ())),
                          preferred_element_type=F32).astype(o_ref.dtype)
    dvp = dv + ONES_ROWS
    ones = jnp.ones((ONES_ROWS, res.shape[1]), o_ref.dtype)
    for h in range(res.shape[0] // dv):
        o_ref[h * dvp:h * dvp + dv, :] = res[h * dv:(h + 1) * dv]
        o_ref[h * dvp + dv:(h + 1) * dvp, :] = ones


def _vt_matmul(w_t, x, *, chunk, tv, dv):
    d_all, k = w_t.shape
    n = x.shape[0]
    tv = min(tv, d_all)
    tvp = tv // dv * (dv + ONES_ROWS)
    return pl.pallas_call(
        functools.partial(_vt_kernel, dv=dv),
        grid=(n // chunk, d_all // tv),
        in_specs=[pl.BlockSpec((tv, k), lambda c, j: (j, 0)),
                  pl.BlockSpec((chunk, k), lambda c, j: (c, 0))],
        out_specs=pl.BlockSpec((None, tvp, chunk), lambda c, j: (c, j, 0)),
        out_shape=jax.ShapeDtypeStruct((n // chunk, d_all // dv * (dv + ONES_ROWS), chunk), BF16),
        compiler_params=_params("arbitrary", "arbitrary"),
        name="vt",
    )(w_t, x)


def _mla_down_kernel(x_ref, w_ref, qn_ref, kvn_ref, cos_ref, sin_ref, cq_ref, cn_ref, kpe_ref):
    acc = jnp.dot(x_ref[...], w_ref[...], preferred_element_type=F32)
    r0, r1 = MLA_Q_RANK, MLA_Q_RANK + MLA_KV_RANK
    cq_ref[...] = _rms(acc[:, :r0], qn_ref[...]).astype(BF16)
    cn_ref[...] = _rms(acc[:, r0:r1], kvn_ref[...]).astype(BF16)
    kpe_ref[...] = _rope(acc[:, r1:], cos_ref[...], sin_ref[...], MLA_ROPE_DIM // 4).astype(BF16)


def _mla_down(h, w_cat, q_norm, kv_norm, cos, sin, *, tm):
    m, k = h.shape
    n = w_cat.shape[1]
    tpb = cos.shape[0] // tm
    row = lambda width: pl.BlockSpec((tm, width), lambda i: (i, 0))
    tab = pl.BlockSpec((tm, LANES), lambda i: (i % tpb, 0))
    return pl.pallas_call(
        _mla_down_kernel,
        grid=(m // tm,),
        in_specs=[row(k),
                  pl.BlockSpec((k, n), lambda i: (0, 0)),
                  pl.BlockSpec((1, MLA_Q_RANK), lambda i: (0, 0)),
                  pl.BlockSpec((1, MLA_KV_RANK), lambda i: (0, 0)),
                  tab, tab],
        out_specs=[row(MLA_Q_RANK), row(MLA_KV_RANK), row(LANES)],
        out_shape=[jax.ShapeDtypeStruct((m, MLA_Q_RANK), BF16),
                   jax.ShapeDtypeStruct((m, MLA_KV_RANK), BF16),
                   jax.ShapeDtypeStruct((m, LANES), BF16)],
        compiler_params=_params("arbitrary"),
        name="mla_down",
    )(h, w_cat, q_norm, kv_norm, cos, sin)


def _mla_kup_kernel(c_ref, w_ref, kpe_ref, o_ref):
    acc = jnp.dot(c_ref[...], w_ref[...], preferred_element_type=F32).astype(BF16)
    kpe = kpe_ref[...]
    for h in range(MLA_HEADS):
        o_ref[:, h * MLA_QK_PAD:h * MLA_QK_PAD + LANES] = acc[:, h * LANES:(h + 1) * LANES]
        o_ref[:, h * MLA_QK_PAD + LANES:(h + 1) * MLA_QK_PAD] = kpe


def _mla_kup(cn, w_uk, kpe, *, tm):
    m, k = cn.shape
    n_out = MLA_HEADS * MLA_QK_PAD
    return pl.pallas_call(
        _mla_kup_kernel,
        grid=(m // tm,),
        in_specs=[pl.BlockSpec((tm, k), lambda i: (i, 0)),
                  pl.BlockSpec((k, MLA_HEADS * MLA_NOPE_DIM), lambda i: (0, 0)),
                  pl.BlockSpec((tm, LANES), lambda i: (i, 0))],
        out_specs=pl.BlockSpec((tm, n_out), lambda i: (i, 0)),
        out_shape=jax.ShapeDtypeStruct((m, n_out), BF16),
        compiler_params=_params("arbitrary"),
        name="mla_kup",
    )(cn, w_uk, kpe)


def _fold_rows(op, s):
    r = s.shape[0]
    while r > 8 and r % 16 == 0:
        r //= 2
        s = op(s[:r], s[r:])
    return s


def _scores(k, q):
    return lax.dot_general(k, q, (((1,), (1,)), ((), ())), preferred_element_type=F32)


def _attend(q_ref, k_ref, vt_ref, o_ref, acc_ref, finish, *, n_maps, n_chunks, chunk, ctx_len, tq):
    tokens, dqk = q_ref.shape
    dv = vt_ref.shape[1] - ONES_ROWS
    dm = dqk // n_maps
    n_lanes = ATTN_STREAMS * n_maps

    def colmax(s):
        return jnp.max(_fold_rows(jnp.maximum, s), axis=0, keepdims=True)

    def split(acc):
        return acc[dv:dv + 1], acc[:dv]

    def pv(v_t, p):
        return jnp.dot(v_t, p.astype(BF16), preferred_element_type=F32)

    parts = []
    for a in range(n_maps):
        s = _scores(k_ref[0:ctx_len, a * dm:(a + 1) * dm], q_ref[0:ctx_len, a * dm:(a + 1) * dm])
        parts.append(split(pv(vt_ref[0, :, 0:ctx_len], jnp.exp2(s - colmax(s)))))
    o_ref[0:ctx_len, :] = finish(parts)

    def q_body(i, _):
        offs = [pl.multiple_of(ctx_len + (i * ATTN_STREAMS + j) * tq, math.gcd(ctx_len, tq))
                for j in range(ATTN_STREAMS)]

        def lane_scores(c, w):
            j, a = divmod(w, n_maps)
            k = k_ref[pl.ds(pl.multiple_of(c * chunk, chunk), chunk), a * dm:(a + 1) * dm]
            return _scores(k, q_ref[pl.ds(offs[j], tq), a * dm:(a + 1) * dm])

        def write_out():
            for j, off in enumerate(offs):
                o_ref[pl.ds(off, tq), :] = finish([split(acc_ref[j * n_maps + a]) for a in range(n_maps)])

        shifts = []
        for w in range(n_lanes):
            s = lane_scores(0, w)
            m = colmax(s)
            acc_ref[w] = pv(vt_ref[0], jnp.exp2(s - m))
            shifts.append(m)

        def fast_body(c, carry):
            shifts, over = carry
            v_t = vt_ref[c]
            out_shifts, out_over = [], []
            for w in range(n_lanes):
                s = lane_scores(c, w)
                m_old = shifts[w]
                cm = colmax(s)
                m_new = jnp.maximum(m_old, cm)
                acc_ref[w] = (acc_ref[w] + pv(v_t, jnp.exp2(s - m_old))) * jnp.exp2(m_old - m_new)
                out_shifts.append(m_new)
                out_over.append(jnp.maximum(over[w], cm - m_old))
            return tuple(out_shifts), tuple(out_over)

        zero = jnp.zeros((1, tq), F32)
        carry = (tuple(shifts), (zero,) * n_lanes)
        n_pairs = (n_chunks - 1) // 2
        carry = lax.fori_loop(0, n_pairs, lambda t, cr: fast_body(2 * t + 2, fast_body(2 * t + 1, cr)), carry)
        if (n_chunks - 1) % 2:
            carry = fast_body(n_chunks - 1, carry)
        _, over = carry
        write_out()

        worst = over[0]
        for w in range(1, n_lanes):
            worst = jnp.maximum(worst, over[w])

        @pl.when(jnp.max(worst) > OVERSHOOT_LIMIT)
        def _():
            def safe_body(c, ms):
                v_t = vt_ref[c]
                out = []
                for w in range(n_lanes):
                    s = lane_scores(c, w)
                    m_new = jnp.maximum(ms[w], colmax(s))
                    acc_ref[w] = acc_ref[w] * jnp.exp2(ms[w] - m_new) + pv(v_t, jnp.exp2(s - m_new))
                    out.append(m_new)
                return tuple(out)

            acc_ref[...] = jnp.zeros(acc_ref.shape, F32)
            lax.fori_loop(0, n_chunks, safe_body, (jnp.full((1, tq), -jnp.inf, F32),) * n_lanes)
            write_out()

        return 0

    lax.fori_loop(0, (tokens - ctx_len) // (tq * ATTN_STREAMS), q_body, 0)


def _mla_attn_kernel(q_ref, k_ref, vt_ref, o_ref, acc_ref, **kw):
    def finish(parts):
        ((l, acc),) = parts
        return (acc / l).T.astype(o_ref.dtype)

    _attend(q_ref, k_ref, vt_ref, o_ref, acc_ref, finish, n_maps=1, **kw)


def _diff_attn_kernel(lam_ref, sub_ref, q_ref, k_ref, vt_ref, o_ref, acc_ref, *, lambda_init, **kw):
    def finish(parts):
        (l1, a1), (l2, a2) = parts
        lp = lam_ref[...]
        lam = (jnp.exp(jnp.sum(lp[0:1] * lp[1:2], keepdims=True))
               - jnp.exp(jnp.sum(lp[2:3] * lp[3:4], keepdims=True)) + lambda_init)
        o = (a1 / l1 - lam * (a2 / l2)).T
        return (_rms(o, sub_ref[...]) * (1.0 - lambda_init)).astype(o_ref.dtype)

    _attend(q_ref, k_ref, vt_ref, o_ref, acc_ref, finish, n_maps=2, **kw)


def _attention(kernel, q, k, vt, *, n_batch, n_heads, n_maps, dqk, dv, ctx_len, tq, chunk, extra=(), name,
               k_block_offset=0):
    n = q.shape[0]
    tokens = n // n_batch
    n_chunks = tokens // chunk
    n_lanes = ATTN_STREAMS * n_maps
    dvp = dv + ONES_ROWS
    extra_specs = [pl.BlockSpec(e.shape, lambda b, h: (0, 0)) for e in extra]
    return pl.pallas_call(
        functools.partial(kernel, n_chunks=n_chunks, chunk=chunk, ctx_len=ctx_len, tq=tq),
        grid=(n_batch, n_heads),
        in_specs=extra_specs + [
            pl.BlockSpec((tokens, dqk), lambda b, h: (b, h)),
            pl.BlockSpec((tokens, dqk), lambda b, h: (b, h + k_block_offset)),
            pl.BlockSpec((n_chunks, dvp, chunk), lambda b, h: (b, h, 0)),
        ],
        out_specs=pl.BlockSpec((tokens, dv), lambda b, h: (b, h)),
        out_shape=jax.ShapeDtypeStruct((n, n_heads * dv), BF16),
        scratch_shapes=[pltpu.VMEM((n_lanes, dvp, tq), F32)],
        compiler_params=_params("arbitrary", "arbitrary"),
        name=name,
    )(*extra, q, k, vt)


def _rope_tables(ctx_len, seq, rot_dim):
    half = rot_dim // 2
    n_freq = half // 2
    t = jnp.arange(seq, dtype=jnp.int32)
    pos_row = (t // GRID_W).astype(F32)
    pos_col = (t % GRID_W).astype(F32)
    inv = ROPE_BASE ** (-jnp.arange(0, half, 2, dtype=F32) / half)
    lane = jnp.arange(LANES)
    freq = inv[(lane % half) % n_freq]
    pos = jnp.where((lane // half)[None, :] == 0, pos_row[:, None], pos_col[:, None])
    ang = pos * freq[None, :]
    live = (lane < rot_dim)[None, :]
    sign = jnp.where((lane % half) < n_freq, -1.0, 1.0)[None, :]
    cos = jnp.where(live, jnp.cos(ang), 1.0)
    sin = jnp.where(live, jnp.sin(ang) * sign, 0.0)
    cos = jnp.concatenate([jnp.ones((ctx_len, LANES), F32), cos], axis=0)
    sin = jnp.concatenate([jnp.zeros((ctx_len, LANES), F32), sin], axis=0)
    return cos, sin


def kernel(x, c, ctx, c_ctx, ada_w, ada_b, norm_g, mla_w_dq, mla_q_norm, mla_w_uq, mla_w_dkv, mla_kv_norm, mla_w_ukv, mla_w_o, diff_w_qkv, diff_lambda, diff_subln, diff_w_o, ffn_w_gu, ffn_w_down, moe_router, moe_w_gu, moe_w_down):
    n_batch, seq, d = x.shape
    ctx_len = ctx.shape[1]
    depth = ada_w.shape[0]
    tokens = ctx_len + seq
    tm = TOKEN_TILE
    assert tokens % tm == 0 and tokens % ctx_len == 0 and ctx_len % LANES == 0 and ctx_len <= tm
    assert n_batch < MOD_ROWS and seq % (ATTN_Q_TILE * ATTN_STREAMS) == 0
    diff_heads = d // (2 * DIFF_HEAD_DIM)

    xs = jnp.concatenate([ctx, x], axis=1).reshape(n_batch * tokens, d)
    cond = jnp.zeros((MOD_ROWS, d), F32).at[:n_batch].set(c).at[n_batch].set(c_ctx)
    ada_b3 = ada_b.reshape(depth, 1, N_MOD * d)
    stream = functools.partial(_stream, n_batch=n_batch, tile=ctx_len)

    cos_d, sin_d = _rope_tables(ctx_len, seq, DIFF_HEAD_DIM)
    cos_m, sin_m = _rope_tables(ctx_len, seq, MLA_ROPE_DIM)

    mods = _ada(cond, ada_w, ada_b3, 0)
    (h,) = stream(xs, g_norm=norm_g[0, 0:1], mods_norm=mods, shift_idx=0)

    for i in range(depth):
        j = i // 2
        g = norm_g[i]
        if i % 2 == 0:
            w_cat = jnp.concatenate(
                [mla_w_dq[j], mla_w_dkv[j], jnp.zeros((d, LANES - MLA_ROPE_DIM), F32)], axis=1).astype(BF16)
            w_uq = mla_w_uq[j].reshape(MLA_Q_RANK, MLA_HEADS, MLA_NOPE_DIM + MLA_ROPE_DIM)
            w_uq = jnp.pad(w_uq, ((0, 0), (0, 0), (0, MLA_QK_PAD - w_uq.shape[2])))
            w_uq = w_uq.reshape(MLA_Q_RANK, MLA_HEADS * MLA_QK_PAD).astype(BF16)
            w_ukv = mla_w_ukv[j].reshape(MLA_KV_RANK, MLA_HEADS, MLA_NOPE_DIM + MLA_V_DIM)
            w_uk = w_ukv[:, :, :MLA_NOPE_DIM].reshape(MLA_KV_RANK, -1).astype(BF16)
            w_uv_t = w_ukv[:, :, MLA_NOPE_DIM:].reshape(MLA_KV_RANK, -1).T.astype(BF16)

            cq, cn, kpe = _mla_down(h, w_cat, mla_q_norm[j][None], mla_kv_norm[j][None], cos_m, sin_m, tm=tm)
            q = _rope_matmul(cq, w_uq, cos_m, sin_m, tm=tm, tn=2048, half=MLA_ROPE_DIM // 4,
                             rope_every=2, scale=MLA_SCALE * LOG2E, name="mla_q")
            k = _mla_kup(cn, w_uk, kpe, tm=tm)
            vt = _vt_matmul(w_uv_t, cn, chunk=tm, tv=w_uv_t.shape[0], dv=MLA_V_DIM)
            o = _attention(_mla_attn_kernel, q, k, vt, n_batch=n_batch, n_heads=MLA_HEADS, n_maps=1,
                           dqk=MLA_QK_PAD, dv=MLA_V_DIM, ctx_len=ctx_len, tq=ATTN_Q_TILE, chunk=tm,
                           name="mla_attn")
            y = _matmul(o, mla_w_o[j].astype(BF16), tm=tm, tn=1024, name="mla_o")
        else:
            lambda_init = 0.8 - 0.6 * math.exp(-0.3 * i)
            w_qk = diff_w_qkv[j][:, :2 * d].astype(BF16)
            w_v_t = diff_w_qkv[j][:, 2 * d:].T.astype(BF16)
            scale = DIFF_HEAD_DIM ** -0.5 * LOG2E
            qk = _rope_matmul(h, w_qk, cos_d, sin_d, tm=tm, tn=1024, half=DIFF_HEAD_DIM // 4,
                              rope_every=1, scale=scale, scaled_cols=d, name="diff_qk")
            vt = _vt_matmul(w_v_t, h, chunk=tm, tv=1024, dv=2 * DIFF_HEAD_DIM)
            o = _attention(functools.partial(_diff_attn_kernel, lambda_init=lambda_init), qk, qk, vt,
                           k_block_offset=diff_heads,
                           n_batch=n_batch, n_heads=diff_heads, n_maps=2, dqk=2 * DIFF_HEAD_DIM,
                           dv=2 * DIFF_HEAD_DIM, ctx_len=ctx_len, tq=ATTN_Q_TILE, chunk=tm,
                           extra=(diff_lambda[j], diff_subln[j][None]), name="diff_attn")
            y = _matmul(o, diff_w_o[j].astype(BF16), tm=tm, tn=1024, name="diff_o")

        if i % 2 == 0:
            xs, h = stream(xs, y=y, g_res=g[1:2], mods_res=mods, gate_idx=2,
                           g_norm=g[2:3], mods_norm=mods, shift_idx=3)
            a = _swiglu_dense(h, ffn_w_gu[j].astype(BF16), tm=tm, tn=512)
            f = _matmul(a, ffn_w_down[j].astype(BF16), tm=tm, tn=1024, name="ffn_down")
        else:
            w_r = jnp.pad(moe_router[j], ((0, 0), (0, LANES - N_EXPERTS)))
            xs, h32, route = stream(xs, y=y, g_res=g[1:2], mods_res=mods, gate_idx=2,
                                    g_norm=g[2:3], mods_norm=mods, shift_idx=3, w_router=w_r)
            n_rows = 2 * xs.shape[0] + N_EXPERTS * MOE_TILE
            src, tile_expert, pos = _route_tables(route, n_rows)
            x_g = _gather_rows(src, h32, out_dtype=BF16, name="moe_dispatch")
            a_g = _moe_up(x_g, moe_w_gu[j].astype(BF16), tile_expert, tn=512)
            y_g = _moe_down(a_g, moe_w_down[j].astype(BF16), tile_expert)
            f = _gather_rows(pos, y_g, out_dtype=BF16, route=route, name="moe_combine")

        if i + 1 < depth:
            mods_next = _ada(cond, ada_w, ada_b3, i + 1)
            xs, h = stream(xs, y=f, g_res=g[3:4], mods_res=mods, gate_idx=5,
                           g_norm=norm_g[i + 1, 0:1], mods_norm=mods_next, shift_idx=0)
            mods = mods_next
        else:
            (out,) = stream(xs, y=f, g_res=g[3:4], mods_res=mods, gate_idx=5, latent_only=True)

    return out.reshape(n_batch, seq, d)
```
